```python
import math
import jax, jax.numpy as jnp
from jax import lax
import numpy as np

D_MODEL = 1024
BATCH = 4
SEQ = 4096
DEPTH = 4

GRID_W = 64
CTX_LEN = 256
MIX_WIDTH = D_MODEL
RWKV_WIDTH = MIX_WIDTH // 2
CONV_WIDTH = MIX_WIDTH - RWKV_WIDTH
HEAD_SIZE = 64
RWKV_HEADS = RWKV_WIDTH // HEAD_SIZE
DECAY_LORA = 64
ICLR_LORA = 64
GATE_LORA = 128
CONV_K = 3
D_FF = 7 * D_MODEL // 2
N_EXPERTS = 8
TOP_K = 2
N_DENSE = (DEPTH + 1) // 2
N_MOE = DEPTH // 2
DN_ALPHA = (2 * DEPTH) ** 0.25
DN_BETA = (8 * DEPTH) ** -0.25
LN_EPS = 1e-5
GN_EPS = 64e-5
NORM_EPS = 1e-12
PROJ_SIZES = (RWKV_WIDTH, RWKV_WIDTH, RWKV_WIDTH, 2 * DECAY_LORA, 2 * ICLR_LORA, GATE_LORA, CONV_WIDTH, CONV_WIDTH, CONV_WIDTH)
PROJ_COLS = sum(PROJ_SIZES)
SPLIT_IDX = tuple(sum(PROJ_SIZES[:i + 1]) for i in range(len(PROJ_SIZES) - 1))
VALUE_GROUPS = (2, 8)

kernel_name = "hybrid_rwkv7_shortconv_moe_deepnorm_dit"


def layer_norm(x, g, b):
    xf = x.astype(jnp.float32)
    mu = jnp.mean(xf, -1, keepdims=True)
    var = jnp.mean(jnp.square(xf - mu), -1, keepdims=True)
    return ((xf - mu) * lax.rsqrt(var + LN_EPS)).astype(x.dtype) * g + b


def heads(t):
    return t.reshape(t.shape[:-1] + (RWKV_HEADS, HEAD_SIZE))


def rwkv_prep(r, k, v, wd, ad, w0, w2, a0, a2, k_k, k_a):
    nb, n = r.shape[:2]
    wd = wd.reshape(nb, n, 2, DECAY_LORA)
    ad = ad.reshape(nb, n, 2, ICLR_LORA)
    w_log = -jax.nn.softplus(-(w0 + jnp.einsum("bldr,drc->bldc", jnp.tanh(wd), w2))) - 0.5
    decay = jnp.exp(-jnp.exp(w_log.astype(jnp.float32)))
    a = jax.nn.sigmoid(a0 + jnp.einsum("bldr,drc->bldc", ad, a2))
    kk = heads(k * k_k).astype(jnp.float32)
    kk = kk / jnp.maximum(jnp.sqrt(jnp.sum(kk * kk, -1, keepdims=True)), NORM_EPS)
    k_dir = heads(k[:, :, None, :] * (1 + (a - 1) * k_a))
    b_vec = kk[:, :, None] * heads(a)
    return (heads(r), k_dir, heads(v), heads(decay), kk, b_vec)


def wkv_scan(s0, terms, d, reverse, readout):
    r, k, v, w, kk, b = terms
    tm = lambda t: jnp.moveaxis(t.astype(jnp.float32), 1, 0)
    xs = (tm(w[:, :, d]), tm(k[:, :, d]), tm(v), tm(kk), tm(b[:, :, d]))
    if readout:
        xs = xs + (tm(r),)

    def step(s, inp):
        w_t, k_t, v_t, kk_t, b_t = inp[:5]
        sa = jnp.einsum("bhvk,bhk->bhv", s, -kk_t)
        s = s * w_t[:, :, None, :] + sa[..., None] * b_t[:, :, None, :] + v_t[..., None] * k_t[:, :, None, :]
        y = jnp.einsum("bhvk,bhk->bhv", s, inp[5]) if readout else None
        return s, y

    s_fin, ys = lax.scan(step, s0, xs, reverse=reverse)
    return s_fin, (jnp.moveaxis(ys, 0, 1) if readout else None)


def rwkv_readout(y, terms, gd, r_k, g2, lnx_g, lnx_b):
    r, k, v = terms[:3]
    mu = jnp.mean(y, -1, keepdims=True)
    var = jnp.mean(jnp.square(y - mu), -1, keepdims=True)
    yn = (y - mu) * lax.rsqrt(var + GN_EPS)
    bonus = jnp.sum(jnp.sum(r[:, :, None] * k * r_k, -1, keepdims=True) * v[:, :, None], axis=2)
    out = yn.reshape(y.shape[:2] + (RWKV_WIDTH,)) * lnx_g + lnx_b + bonus.reshape(y.shape[:2] + (RWKV_WIDTH,))
    g = jax.nn.sigmoid(gd) @ g2
    return (out * g).astype(gd.dtype)


def short_conv(u, w):
    pad = CONV_K // 2
    n = u.shape[-2]
    up = jnp.pad(u, [(0, 0)] * (u.ndim - 2) + [(pad, pad), (0, 0)])
    out = up[..., 0:n, :] * w[0]
    for j in range(1, CONV_K):
        out = out + up[..., j:j + n, :] * w[j]
    return out


def conv_branch(cb, cc, ch, conv_w, rows):
    u = cc * ch
    if rows is not None:
        u = u.reshape(u.shape[0], rows, GRID_W, u.shape[-1])
    return cb * short_conv(u, conv_w).reshape(cb.shape)


def token_mixer(hl, hc, rows, w_in, w0, w2, a0, a2, k_k, k_a, r_k, g2, lnx_g, lnx_b, conv_w, w_out, ctx_out):
    pl = jnp.split(hl @ w_in, SPLIT_IDX, axis=-1)
    pc = jnp.split(hc @ w_in, SPLIT_IDX, axis=-1)
    lat = rwkv_prep(pl[0], pl[1], pl[2], pl[3], pl[4], w0, w2, a0, a2, k_k, k_a)
    cx = rwkv_prep(pc[0], pc[1], pc[2], pc[3], pc[4], w0, w2, a0, a2, k_k, k_a)
    s0 = jnp.zeros((hl.shape[0], RWKV_HEADS, HEAD_SIZE, HEAD_SIZE), jnp.float32)
    y_lat, y_ctx = 0.0, 0.0
    for d in range(2):
        s_ctx, yc = wkv_scan(s0, cx, d, d == 1, ctx_out)
        _, yl = wkv_scan(s_ctx, lat, d, d == 1, True)
        y_lat = y_lat + yl
        if ctx_out:
            y_ctx = y_ctx + yc
    out_l = jnp.concatenate([rwkv_readout(y_lat, lat, pl[5], r_k, g2, lnx_g, lnx_b),
                             conv_branch(pl[6], pl[7], pl[8], conv_w, rows)], axis=-1) @ w_out
    if not ctx_out:
        return out_l, None
    out_c = jnp.concatenate([rwkv_readout(y_ctx, cx, pc[5], r_k, g2, lnx_g, lnx_b),
                             conv_branch(pc[6], pc[7], pc[8], conv_w, None)], axis=-1) @ w_out
    return out_l, out_c


def swiglu(x, w1, w3, w2):
    return (jax.nn.silu(x @ w1) * (x @ w3)) @ w2


def moe_ffn(x, w_router, b_router, w1, w3, w2):
    logits = (x @ w_router + b_router).astype(jnp.float32)
    top_v, top_i = lax.top_k(logits, TOP_K)
    top_w = jax.nn.softmax(top_v, axis=-1)
    gate = jnp.sum(jax.nn.one_hot(top_i, N_EXPERTS, dtype=jnp.float32) * top_w[..., None], axis=-2).astype(x.dtype)
    out = gate[..., 0:1] * swiglu(x, w1[0], w3[0], w2[0])
    for e in range(1, N_EXPERTS):
        out = out + gate[..., e:e + 1] * swiglu(x, w1[e], w3[e], w2[e])
    return out


def channel_mixer(h, l, ffn_w1, ffn_w3, ffn_w2, router_w, router_b, moe_w1, moe_w3, moe_w2):
    i = l // 2
    if l % 2 == 0:
        return swiglu(h, ffn_w1[i], ffn_w3[i], ffn_w2[i])
    return moe_ffn(h, router_w[i], router_b[i], moe_w1[i], moe_w3[i], moe_w2[i])


def setup_inputs(seed: int = 0) -> dict:
    key = jax.random.key(seed)
    ks = iter(jax.random.split(key, 40))
    D = D_MODEL
    L = DEPTH
    nrm = lambda shape, s: jax.random.normal(next(ks), shape, jnp.float32) * s
    col_scale = jnp.concatenate([jnp.full((n,), DN_BETA if i in VALUE_GROUPS else 1.0, jnp.float32)
                                 for i, n in enumerate(PROJ_SIZES)])
    return {
        "x": nrm((BATCH, SEQ, D), 1.0),
        "c": nrm((BATCH, D), 1.0),
        "ctx": nrm((BATCH, CTX_LEN, D), 1.0),
        "c_ctx": nrm((D,), 1.0),
        "w_mod": nrm((L, D, 6 * D), 0.5 * D ** -0.5),
        "b_mod": nrm((L, 6 * D), 0.02),
        "w_in": nrm((L, D, PROJ_COLS), D ** -0.5) * col_scale,
        "rk_w0": -6.0 + 5.0 * jax.random.uniform(next(ks), (L, 2, RWKV_WIDTH), jnp.float32),
        "rk_w2": nrm((L, 2, DECAY_LORA, RWKV_WIDTH), 0.1 * DECAY_LORA ** -0.5),
        "rk_a0": nrm((L, 2, RWKV_WIDTH), 0.1),
        "rk_a2": nrm((L, 2, ICLR_LORA, RWKV_WIDTH), 0.1 * ICLR_LORA ** -0.5),
        "rk_kk": 0.85 + nrm((L, RWKV_WIDTH), 0.02),
        "rk_ka": 1.0 + nrm((L, RWKV_WIDTH), 0.02),
        "rk_rk": nrm((L, RWKV_HEADS, HEAD_SIZE), 0.1),
        "rk_g2": nrm((L, GATE_LORA, RWKV_WIDTH), GATE_LORA ** -0.5),
        "rk_lnx_g": 1.0 + nrm((L, RWKV_WIDTH), 0.02),
        "rk_lnx_b": nrm((L, RWKV_WIDTH), 0.02),
        "conv_w": nrm((L, CONV_K, CONV_WIDTH), CONV_K ** -0.5),
        "w_out": nrm((L, MIX_WIDTH, D), MIX_WIDTH ** -0.5 * DN_BETA),
        "ln1_g": 1.0 + nrm((L, D), 0.02),
        "ln1_b": nrm((L, D), 0.02),
        "ln2_g": 1.0 + nrm((L, D), 0.02),
        "ln2_b": nrm((L, D), 0.02),
        "ffn_w1": nrm((N_DENSE, D, D_FF), D ** -0.5),
        "ffn_w3": nrm((N_DENSE, D, D_FF), D ** -0.5),
        "ffn_w2": nrm((N_DENSE, D_FF, D), D_FF ** -0.5 * DN_BETA),
        "router_w": nrm((N_MOE, D, N_EXPERTS), D ** -0.5),
        "router_b": nrm((N_MOE, N_EXPERTS), 0.01),
        "moe_w1": nrm((N_MOE, N_EXPERTS, D, D_FF), D ** -0.5),
        "moe_w3": nrm((N_MOE, N_EXPERTS, D, D_FF), D ** -0.5),
        "moe_w2": nrm((N_MOE, N_EXPERTS, D_FF, D), D_FF ** -0.5 * DN_BETA),
    }


def reference(x, c, ctx, c_ctx, w_mod, b_mod, w_in, rk_w0, rk_w2, rk_a0, rk_a2, rk_kk, rk_ka, rk_rk, rk_g2,
              rk_lnx_g, rk_lnx_b, conv_w, w_out, ln1_g, ln1_b, ln2_g, ln2_b, ffn_w1, ffn_w3, ffn_w2,
              router_w, router_b, moe_w1, moe_w3, moe_w2):
    rows = x.shape[1] // GRID_W
    n_ctx = ctx.shape[1]
    xl, xc = x, ctx
    for l in range(DEPTH):
        last = l == DEPTH - 1
        ml = jnp.split((jax.nn.silu(c) @ w_mod[l] + b_mod[l])[:, None, :], 6, axis=-1)
        mc = jnp.split(jax.nn.silu(c_ctx) @ w_mod[l] + b_mod[l], 6, axis=-1)
        mix_l, mix_c = token_mixer(xl * (1 + ml[1]) + ml[0], xc * (1 + mc[1]) + mc[0], rows,
                                   w_in[l], rk_w0[l], rk_w2[l], rk_a0[l], rk_a2[l], rk_kk[l], rk_ka[l],
                                   rk_rk[l], rk_g2[l], rk_lnx_g[l], rk_lnx_b[l], conv_w[l], w_out[l], not last)
        xl = layer_norm(DN_ALPHA * xl + ml[2] * mix_l, ln1_g[l], ln1_b[l])
        if last:
            f_l = channel_mixer(xl * (1 + ml[4]) + ml[3], l, ffn_w1, ffn_w3, ffn_w2,
                                router_w, router_b, moe_w1, moe_w3, moe_w2)
        else:
            xc = layer_norm(DN_ALPHA * xc + mc[2] * mix_c, ln1_g[l], ln1_b[l])
            h = jnp.concatenate([xc * (1 + mc[4]) + mc[3], xl * (1 + ml[4]) + ml[3]], axis=1)
            f = channel_mixer(h, l, ffn_w1, ffn_w3, ffn_w2, router_w, router_b, moe_w1, moe_w3, moe_w2)
            xc = layer_norm(DN_ALPHA * xc + mc[5] * f[:, :n_ctx], ln2_g[l], ln2_b[l])
            f_l = f[:, n_ctx:]
        xl = layer_norm(DN_ALPHA * xl + ml[5] * f_l, ln2_g[l], ln2_b[l])
    return xl
```

```python
import functools

import jax
import jax.numpy as jnp
from jax import lax
from jax.experimental import pallas as pl
from jax.experimental.pallas import tpu as pltpu

F32 = jnp.float32
BF16 = jnp.bfloat16
HIGHEST = lax.Precision.HIGHEST

HEAD = 64
GRID_W = 64
CONV_K = 3
TOP_K = 2
LN_EPS = 1e-5
GN_EPS = 64e-5
NORM_EPS = 1e-12
PAD_LOGIT = -1e30
MXU_WIDTH_V7X = 256
GROUP = MXU_WIDTH_V7X // HEAD
CHUNK = MXU_WIDTH_V7X // GROUP
TOK_BLOCK = 256
LANE = 128
VMEM_LIMIT_V7X = 56 * 1024 * 1024
FFN_M_BLOCKS_PER_BATCH = 4
FFN_F_BLOCK = 512
MOD_N_BLOCK = 1536


def _mm(a, b):
    return jnp.dot(a.astype(BF16), b.astype(BF16), preferred_element_type=F32)


def _mm_nt(a, b):
    return lax.dot_general(a.astype(BF16), b.astype(BF16), (((1,), (1,)), ((), ())),
                           preferred_element_type=F32)


def _sigmoid(x):
    return 1.0 / (1.0 + jnp.exp(-x))


def _layer_norm(z, g, b):
    mu = jnp.mean(z, axis=-1, keepdims=True)
    d = z - mu
    var = jnp.mean(d * d, axis=-1, keepdims=True)
    return d * lax.rsqrt(var + LN_EPS) * g + b


def _params(sem):
    return pltpu.CompilerParams(dimension_semantics=sem, vmem_limit_bytes=VMEM_LIMIT_V7X)


def _mod_kernel(c_ref, w_ref, b_ref, o_ref):
    c = c_ref[...]
    s = c * _sigmoid(c)
    o_ref[...] = jnp.dot(s, w_ref[...], precision=HIGHEST, preferred_element_type=F32) + b_ref[...]


def _modulation(cc, w_mod, b_mod):
    depth, d, n = w_mod.shape
    tn = MOD_N_BLOCK
    return pl.pallas_call(
        _mod_kernel,
        grid=(depth, n // tn),
        in_specs=[pl.BlockSpec((8, d), lambda l, j: (0, 0)),
                  pl.BlockSpec((None, d, tn), lambda l, j: (l, 0, j)),
                  pl.BlockSpec((None, 1, tn), lambda l, j: (l, 0, j))],
        out_specs=pl.BlockSpec((None, 8, tn), lambda l, j: (l, 0, j)),
        out_shape=jax.ShapeDtypeStruct((depth, 8, n), F32),
        compiler_params=_params(("parallel", "parallel")),
        name="adaln_modulation",
    )(cc, w_mod, b_mod.reshape(depth, 1, n))


def _inproj_kernel(x_ref, mod_ref, w_ref, w2s_ref, a2s_ref, w0_ref, a0_ref, kk_ref, ka_ref, rk_ref, g2_ref,
                   cw_ref, ones_ref, sh_ref, dir_ref, ex_ref, *, width, blocks_per_batch):
    wd_ = width
    x = x_ref[...]
    h = x * (1.0 + mod_ref[1:2, :]) + mod_ref[0:1, :]
    p = _mm(h, w_ref[...])
    r = p[:, 0:wd_]
    k = p[:, wd_:2 * wd_]
    v = p[:, 2 * wd_:3 * wd_]
    o = 3 * wd_
    wd = p[:, o:o + LANE]
    ad = p[:, o + LANE:o + 2 * LANE]
    gd = p[:, o + 2 * LANE:o + 3 * LANE]
    o = o + 3 * LANE
    cb = p[:, o:o + wd_]
    cc = p[:, o + wd_:o + 2 * wd_]
    ch = p[:, o + 2 * wd_:o + 3 * wd_]
    ones = ones_ref[...]

    z = w0_ref[...] + _mm(jnp.tanh(wd), w2s_ref[...])
    softplus_neg = jnp.maximum(-z, 0.0) + jnp.log(1.0 + jnp.exp(-jnp.abs(z)))
    w_log = -softplus_neg - 0.5
    log_decay = -jnp.exp(w_log)
    a_all = _sigmoid(a0_ref[...] + _mm(ad, a2s_ref[...]))

    kk_raw = k * kk_ref[...]
    sq = kk_raw * kk_raw
    sq_hi = sq.astype(BF16)
    sq_lo = (sq - sq_hi.astype(F32)).astype(BF16)
    ss = (jnp.dot(sq_hi, ones, preferred_element_type=F32) + jnp.dot(sq_lo, ones, preferred_element_type=F32))
    kk = kk_raw / jnp.maximum(jnp.sqrt(ss), NORM_EPS)

    sh_ref[:, 0:wd_] = r
    sh_ref[:, wd_:2 * wd_] = kk
    sh_ref[:, 2 * wd_:3 * wd_] = v
    kd_sum = jnp.zeros_like(k)
    for d in range(2):
        a = a_all[:, d * wd_:(d + 1) * wd_]
        kd = k * (1.0 + (a - 1.0) * ka_ref[...])
        kd_sum = kd_sum + kd
        dir_ref[d, :, 0:wd_] = log_decay[:, d * wd_:(d + 1) * wd_]
        dir_ref[d, :, wd_:2 * wd_] = kd
        dir_ref[d, :, 2 * wd_:3 * wd_] = kk * a
    bonus = _mm(r * kd_sum * rk_ref[...], ones) * v
    gate = _mm(_sigmoid(gd), g2_ref[...])

    u = cc * ch
    rows = lax.broadcasted_iota(jnp.int32, u.shape, 0)
    is_ctx = (pl.program_id(0) % blocks_per_batch) == 0
    period_mask = jnp.where(is_ctx, TOK_BLOCK - 1, GRID_W - 1)
    pos = rows & period_mask
    u_prev = jnp.where(pos == 0, 0.0, pltpu.roll(u, 1, 0))
    u_next = jnp.where(pos == period_mask, 0.0, pltpu.roll(u, TOK_BLOCK - 1, 0))
    conv = cb * (u_prev * cw_ref[0:1, :] + u * cw_ref[1:2, :] + u_next * cw_ref[2:3, :])

    ex_ref[:, 0:wd_] = bonus
    ex_ref[:, wd_:2 * wd_] = gate
    ex_ref[:, 2 * wd_:3 * wd_] = conv


def _inproj(x, modtab, w_in, w2s, a2s, w0, a0, k_k, k_a, r_k, g2, conv_w, ones, *, blocks_per_batch):
    ntok, d = x.shape
    width = k_k.shape[-1]
    pcols = w_in.shape[1]
    nb = ntok // TOK_BLOCK
    const = lambda *shape: pl.BlockSpec(shape, lambda i: (0,) * len(shape))
    return pl.pallas_call(
        functools.partial(_inproj_kernel, width=width, blocks_per_batch=blocks_per_batch),
        grid=(nb,),
        in_specs=[pl.BlockSpec((TOK_BLOCK, d), lambda i: (i, 0)),
                  pl.BlockSpec((None, None, 6, d),
                               lambda i: (i // blocks_per_batch, jnp.minimum(i % blocks_per_batch, 1), 0, 0)),
                  const(d, pcols), const(LANE, 2 * width), const(LANE, 2 * width), const(1, 2 * width),
                  const(1, 2 * width), const(1, width), const(1, width), const(1, width), const(LANE, width),
                  const(CONV_K, width), const(width, width)],
        out_specs=[pl.BlockSpec((TOK_BLOCK, 3 * width), lambda i: (i, 0)),
                   pl.BlockSpec((2, TOK_BLOCK, 3 * width), lambda i: (0, i, 0)),
                   pl.BlockSpec((TOK_BLOCK, 3 * width), lambda i: (i, 0))],
        out_shape=[jax.ShapeDtypeStruct((ntok, 3 * width), F32),
                   jax.ShapeDtypeStruct((2, ntok, 3 * width), F32),
                   jax.ShapeDtypeStruct((ntok, 3 * width), F32)],
        compiler_params=_params(("parallel",)),
        name="inproj_prep",
    )(x, modtab, w_in, w2s, a2s, w0, a0, k_k, k_a, r_k, g2, conv_w, ones)


def _scan_masks(rev):
    n = GROUP * CHUNK
    ri = lax.broadcasted_iota(jnp.int32, (n, n), 0)
    ci = lax.broadcasted_iota(jnp.int32, (n, n), 1)
    same_head = (ri // CHUNK) == (ci // CHUNK)
    tr = ri % CHUNK
    tc = ci % CHUNK
    before = (tc > tr) if rev else (tc < tr)
    strict = same_head & before
    incl = same_head & (before | (tc == tr))
    eye = ri == ci
    levels = []
    s = 1
    while s < CHUNK:
        levels.append(((tr // (2 * s)) == (tc // (2 * s))) & ((tr // s) != (tc // s)))
        s *= 2
    return same_head, strict, incl, eye, levels


def _scan_chunk(r, kk, v, lw, kd, b, s_prev, masks, rev):
    same_head, strict, incl, eye, levels = masks
    ti = lax.broadcasted_iota(jnp.int32, (CHUNK, CHUNK), 0)
    si = lax.broadcasted_iota(jnp.int32, (CHUNK, CHUNK), 1)
    tri = jnp.where((si >= ti) if rev else (si <= ti), 1.0, 0.0).astype(F32)
    g_inc = jnp.dot(tri, lw, precision=HIGHEST, preferred_element_type=F32)
    g_exc = g_inc - lw
    g_end = g_inc[0:1, :] if rev else g_inc[CHUNK - 1:CHUNK, :]
    e_neg = jnp.exp(-g_inc)
    e_end = jnp.exp(g_end - g_inc)

    def blk(m):
        return jnp.where(same_head, jnp.concatenate([m] * GROUP, axis=0), 0.0)

    kk_b = blk(kk * jnp.exp(g_exc)).astype(BF16)
    r_b = blk(r * jnp.exp(g_inc)).astype(BF16)
    bt_b = blk(b * e_neg).astype(BF16)
    kt_b = blk(kd * e_neg).astype(BF16)
    bend_b = blk(b * e_end).astype(BF16)
    kend_b = blk(kd * e_end).astype(BF16)
    v_f = blk(v)
    v_b = v_f.astype(BF16)
    vt_b = v_f.T.astype(BF16)

    n = GROUP * CHUNK
    pair = _mm_nt(jnp.concatenate([kk_b, r_b], axis=0), jnp.concatenate([bt_b, kt_b], axis=0))
    l_b = jnp.where(strict, pair[0:n, 0:n], 0.0)
    l_k = jnp.where(strict, pair[0:n, n:2 * n], 0.0).astype(BF16)
    a_rb = jnp.where(incl, pair[n:2 * n, 0:n], 0.0).astype(BF16)
    a_rk = jnp.where(incl, pair[n:2 * n, n:2 * n], 0.0).astype(BF16)

    inv = jnp.where(eye, 1.0, 0.0) - jnp.where(levels[0], l_b, 0.0)
    for lvl in levels[1:]:
        inv_b = inv.astype(BF16)
        c_b = jnp.where(lvl, l_b, 0.0).astype(BF16)
        inv = inv - _mm(inv_b, _mm(c_b, inv_b))

    s_b = s_prev.astype(BF16)
    x = -(_mm_nt(kk_b, s_b) + _mm(l_k, v_b))
    u = _mm(inv, x)
    y_b = _mm_nt(r_b, s_b) + _mm(jnp.concatenate([a_rb, a_rk], axis=1),
                                 jnp.concatenate([u.astype(BF16), v_b], axis=0))
    s_new = jnp.exp(g_end) * s_prev + _mm(jnp.concatenate([u.T.astype(BF16), vt_b], axis=1),
                                          jnp.concatenate([bend_b, kend_b], axis=0))
    y = y_b[0:CHUNK]
    for hh in range(1, GROUP):
        y = y + y_b[hh * CHUNK:(hh + 1) * CHUNK]
    return y, s_new


def _scan_kernel(shf_ref, shb_ref, df_ref, db_ref, yf_ref, yb_ref, s_ref, *, width):
    @pl.when(pl.program_id(1) == 0)
    def _():
        s_ref[...] = jnp.zeros_like(s_ref)

    gw = GROUP * HEAD
    n_groups = width // gw
    for d, (sh, dr, yo) in enumerate(((shf_ref, df_ref, yf_ref), (shb_ref, db_ref, yb_ref))):
        rev = d == 1
        masks = _scan_masks(rev)
        for g in range(n_groups):
            lo = g * gw
            y, s_new = _scan_chunk(sh[:, lo:lo + gw], sh[:, width + lo:width + lo + gw],
                                   sh[:, 2 * width + lo:2 * width + lo + gw],
                                   dr[:, lo:lo + gw], dr[:, width + lo:width + lo + gw],
                                   dr[:, 2 * width + lo:2 * width + lo + gw],
                                   s_ref[d * n_groups + g], masks, rev)
            yo[:, lo:lo + gw] = y
            s_ref[d * n_groups + g] = s_new


def _wkv_scan(shared, per_dir, *, batch, ctx_chunks):
    ntok, w3 = shared.shape
    width = w3 // 3
    nc = ntok // batch // CHUNK
    n_groups = width // (GROUP * HEAD)

    def fwd(b, i):
        return b * nc + i

    def bwd(b, i):
        return b * nc + jnp.where(i < ctx_chunks, ctx_chunks - 1 - i, nc - 1 - (i - ctx_chunks))

    return pl.pallas_call(
        functools.partial(_scan_kernel, width=width),
        grid=(batch, nc),
        in_specs=[pl.BlockSpec((CHUNK, w3), lambda b, i: (fwd(b, i), 0)),
                  pl.BlockSpec((CHUNK, w3), lambda b, i: (bwd(b, i), 0)),
                  pl.BlockSpec((None, CHUNK, w3), lambda b, i: (0, fwd(b, i), 0)),
                  pl.BlockSpec((None, CHUNK, w3), lambda b, i: (1, bwd(b, i), 0))],
        out_specs=[pl.BlockSpec((CHUNK, width), lambda b, i: (fwd(b, i), 0)),
                   pl.BlockSpec((CHUNK, width), lambda b, i: (bwd(b, i), 0))],
        out_shape=[jax.ShapeDtypeStruct((ntok, width), F32)] * 2,
        scratch_shapes=[pltpu.VMEM((2 * n_groups, GROUP * HEAD, GROUP * HEAD), F32)],
        compiler_params=_params(("parallel", "arbitrary")),
        name="wkv_scan",
    )(shared, shared, per_dir, per_dir)


def _top2_gates(logits):
    lane = lax.broadcasted_iota(jnp.int32, logits.shape, 1)
    n = logits.shape[-1]
    m1 = jnp.max(logits, axis=-1, keepdims=True)
    i1 = jnp.min(jnp.where(logits == m1, lane, n), axis=-1, keepdims=True)
    rest = jnp.where(lane == i1, PAD_LOGIT, logits)
    m2 = jnp.max(rest, axis=-1, keepdims=True)
    i2 = jnp.min(jnp.where(rest == m2, lane, n), axis=-1, keepdims=True)
    e = jnp.exp(m2 - m1)
    return jnp.where(lane == i1, 1.0 / (1.0 + e), 0.0) + jnp.where(lane == i2, e / (1.0 + e), 0.0)


def _outproj_kernel(*refs, width, alpha, moe):
    if moe:
        (yf_ref, yb_ref, ex_ref, x_ref, mod_ref, lg_ref, lb_ref, wo_ref, g1_ref, b1_ref, ones_ref,
         rw_ref, rb_ref, x1_ref, h2_ref, gate_ref) = refs
    else:
        (yf_ref, yb_ref, ex_ref, x_ref, mod_ref, lg_ref, lb_ref, wo_ref, g1_ref, b1_ref, ones_ref,
         x1_ref, h2_ref) = refs
    ones = ones_ref[...]
    y = yf_ref[...] + yb_ref[...]
    inv_n = 1.0 / HEAD
    mu = _mm(y, ones) * inv_n
    dlt = y - mu
    var = _mm(dlt * dlt, ones) * inv_n
    yn = dlt * lax.rsqrt(var + GN_EPS)
    bonus = ex_ref[:, 0:width]
    gate = ex_ref[:, width:2 * width]
    conv = ex_ref[:, 2 * width:3 * width]
    rwkv = (yn * lg_ref[...] + lb_ref[...] + bonus) * gate
    cat = jnp.concatenate([rwkv.astype(BF16), conv.astype(BF16)], axis=1)
    mix = jnp.dot(cat, wo_ref[...], preferred_element_type=F32)
    x1 = _layer_norm(alpha * x_ref[...] + mod_ref[2:3, :] * mix, g1_ref[...], b1_ref[...])
    h2 = x1 * (1.0 + mod_ref[4:5, :]) + mod_ref[3:4, :]
    x1_ref[...] = x1
    h2_ref[...] = h2.astype(BF16)
    if moe:
        logits = jnp.dot(h2, rw_ref[...], precision=HIGHEST, preferred_element_type=F32) + rb_ref[...]
        gate_ref[...] = _top2_gates(logits)


def _outproj(yf, yb, extras, x, modtab, lnx_g, lnx_b, w_out, ln_g, ln_b, ones, router, *, alpha,
             blocks_per_batch):
    ntok, d = x.shape
    width = yf.shape[1]
    nb = ntok // TOK_BLOCK
    moe = router is not None
    const = lambda *shape: pl.BlockSpec(shape, lambda i: (0,) * len(shape))
    row = lambda cols: pl.BlockSpec((TOK_BLOCK, cols), lambda i: (i, 0))
    in_specs = [row(width), row(width), row(3 * width), row(d),
                pl.BlockSpec((None, None, 6, d),
                             lambda i: (i // blocks_per_batch, jnp.minimum(i % blocks_per_batch, 1), 0, 0)),
                const(1, width), const(1, width), const(2 * width, d), const(1, d), const(1, d),
                const(width, width)]
    out_specs = [row(d), row(d)]
    out_shape = [jax.ShapeDtypeStruct((ntok, d), F32), jax.ShapeDtypeStruct((ntok, d), BF16)]
    args = [yf, yb, extras, x, modtab, lnx_g, lnx_b, w_out, ln_g, ln_b, ones]
    if moe:
        in_specs += [const(d, LANE), const(1, LANE)]
        out_specs.append(row(LANE))
        out_shape.append(jax.ShapeDtypeStruct((ntok, LANE), F32))
        args += list(router)
    return pl.pallas_call(
        functools.partial(_outproj_kernel, width=width, alpha=alpha, moe=moe),
        grid=(nb,), in_specs=in_specs, out_specs=out_specs, out_shape=out_shape,
        compiler_params=_params(("parallel",)),
        name="readout_outproj",
    )(*args)


def _ffn_kernel(*refs, alpha, moe, ctx_len, m_blocks_per_batch):
    if moe:
        h_ref, x1_ref, gates_ref, w1_ref, w3_ref, w2_ref, mod_ref, g_ref, b_ref, o_ref, acc_ref = refs
    else:
        h_ref, x1_ref, w1_ref, w3_ref, w2_ref, mod_ref, g_ref, b_ref, o_ref, acc_ref = refs
    e = pl.program_id(1)
    f = pl.program_id(2)

    @pl.when((e == 0) & (f == 0))
    def _():
        acc_ref[...] = jnp.zeros_like(acc_ref)

    h = h_ref[...]
    a = jnp.dot(h, w1_ref[...], preferred_element_type=F32)
    b = jnp.dot(h, w3_ref[...], preferred_element_type=F32)
    s = a * _sigmoid(a) * b
    if moe:
        lane = lax.broadcasted_iota(jnp.int32, gates_ref.shape, 1)
        s = s * jnp.sum(jnp.where(lane == e, gates_ref[...], 0.0), axis=-1, keepdims=True)
    acc_ref[...] += jnp.dot(s.astype(BF16), w2_ref[...], preferred_element_type=F32)

    @pl.when((e == pl.num_programs(1) - 1) & (f == pl.num_programs(2) - 1))
    def _():
        tm = acc_ref.shape[0]
        rows = lax.broadcasted_iota(jnp.int32, acc_ref.shape, 0)
        row_in_batch = rows + (pl.program_id(0) % m_blocks_per_batch) * tm
        gate2 = jnp.where(row_in_batch < ctx_len, mod_ref[0, 5:6, :], mod_ref[1, 5:6, :])
        o_ref[...] = _layer_norm(alpha * x1_ref[...] + gate2 * acc_ref[...], g_ref[...], b_ref[...])


def _ffn(h2, x1, gates, w1, w3, w2, modtab, ln_g, ln_b, *, alpha, ctx_len, batch):
    ntok, d = x1.shape
    n_exp, _, dff = w1.shape
    mpb = FFN_M_BLOCKS_PER_BATCH
    tm = ntok // batch // mpb
    tf = min(FFN_F_BLOCK, dff)
    moe = gates is not None
    row = lambda cols: pl.BlockSpec((tm, cols), lambda m, e, f: (m, 0))
    const = lambda *shape: pl.BlockSpec(shape, lambda m, e, f: (0,) * len(shape))
    in_specs = [row(d), row(d)] + ([row(LANE)] if moe else []) + [
        pl.BlockSpec((None, d, tf), lambda m, e, f: (e, 0, f)),
        pl.BlockSpec((None, d, tf), lambda m, e, f: (e, 0, f)),
        pl.BlockSpec((None, tf, d), lambda m, e, f: (e, f, 0)),
        pl.BlockSpec((None, 2, 6, d), lambda m, e, f: (m // mpb, 0, 0, 0)),
        const(1, d), const(1, d)]
    args = [h2, x1] + ([gates] if moe else []) + [w1, w3, w2, modtab, ln_g, ln_b]
    return pl.pallas_call(
        functools.partial(_ffn_kernel, alpha=alpha, moe=moe, ctx_len=ctx_len, m_blocks_per_batch=mpb),
        grid=(ntok // tm, n_exp, dff // tf),
        in_specs=in_specs, out_specs=row(d),
        out_shape=jax.ShapeDtypeStruct((ntok, d), F32),
        scratch_shapes=[pltpu.VMEM((tm, d), F32)],
        compiler_params=_params(("parallel", "arbitrary", "arbitrary")),
        name="moe_swiglu" if moe else "dense_swiglu",
    )(*args)


def _head_ones(width):
    hid = jnp.arange(width) // HEAD
    return (hid[:, None] == hid[None, :]).astype(BF16)


def _stack_lora(w):
    _, rank, width = w.shape
    z = jnp.zeros((rank, width), w.dtype)
    return jnp.concatenate([jnp.concatenate([w[0], z], axis=1), jnp.concatenate([z, w[1]], axis=1)], axis=0)


def kernel(x, c, ctx, c_ctx, w_mod, b_mod, w_in, rk_w0, rk_w2, rk_a0, rk_a2, rk_kk, rk_ka, rk_rk, rk_g2,
           rk_lnx_g, rk_lnx_b, conv_w, w_out, ln1_g, ln1_b, ln2_g, ln2_b, ffn_w1, ffn_w3, ffn_w2,
           router_w, router_b, moe_w1, moe_w3, moe_w2):
    batch, seq, d = x.shape
    ctx_len = ctx.shape[1]
    depth = w_in.shape[0]
    width = rk_kk.shape[-1]
    n_exp = router_w.shape[-1]
    tok = ctx_len + seq
    assert ctx_len == TOK_BLOCK and seq % TOK_BLOCK == 0 and seq % GRID_W == 0 and batch < 8
    assert 2 * rk_w2.shape[2] == LANE and 2 * rk_a2.shape[2] == LANE and rk_g2.shape[1] == LANE
    assert width % (GROUP * HEAD) == 0 and n_exp <= LANE and tok % (FFN_M_BLOCKS_PER_BATCH * 16) == 0
    blocks_per_batch = tok // TOK_BLOCK
    alpha = (2 * depth) ** 0.25

    cc = jnp.concatenate([c, c_ctx[None, :], jnp.zeros((8 - batch - 1, d), F32)], axis=0)
    mods = _modulation(cc, w_mod, b_mod).reshape(depth, 8, 6, d)
    modtab = jnp.stack([jnp.broadcast_to(mods[:, batch][:, None], (depth, batch, 6, d)), mods[:, :batch]],
                       axis=2)

    xs = jnp.concatenate([ctx, x], axis=1).reshape(batch * tok, d)
    ones = _head_ones(width)
    for l in range(depth):
        shared, per_dir, extras = _inproj(
            xs, modtab[l], w_in[l].astype(BF16), _stack_lora(rk_w2[l]).astype(BF16),
            _stack_lora(rk_a2[l]).astype(BF16), rk_w0[l].reshape(1, 2 * width), rk_a0[l].reshape(1, 2 * width),
            rk_kk[l][None], rk_ka[l][None], rk_rk[l].reshape(1, width), rk_g2[l].astype(BF16), conv_w[l], ones,
            blocks_per_batch=blocks_per_batch)
        yf, yb = _wkv_scan(shared, per_dir, batch=batch, ctx_chunks=ctx_len // CHUNK)
        i = l // 2
        router = None
        if l % 2 == 1:
            router = (jnp.pad(router_w[i], ((0, 0), (0, LANE - n_exp))),
                      jnp.pad(router_b[i], (0, LANE - n_exp), constant_values=PAD_LOGIT)[None])
        res = _outproj(yf, yb, extras, xs, modtab[l], rk_lnx_g[l][None], rk_lnx_b[l][None],
                       w_out[l].astype(BF16), ln1_g[l][None], ln1_b[l][None], ones, router,
                       alpha=alpha, blocks_per_batch=blocks_per_batch)
        if l % 2 == 0:
            x1, h2 = res
            gates = None
            w1, w3, w2 = ffn_w1[i][None], ffn_w3[i][None], ffn_w2[i][None]
        else:
            x1, h2, gates = res
            w1, w3, w2 = moe_w1[i], moe_w3[i], moe_w2[i]
        xs = _ffn(h2, x1, gates, w1.astype(BF16), w3.astype(BF16), w2.astype(BF16), modtab[l],
                  ln2_g[l][None], ln2_b[l][None], alpha=alpha, ctx_len=ctx_len, batch=batch)
    return xs.reshape(batch, tok, d)[:, ctx_len:]
```

```python
import functools

import jax
import jax.numpy as jnp
from jax import lax
from jax.experimental import pallas as pl
from jax.experimental.pallas import tpu as pltpu

F32 = jnp.float32
BF16 = jnp.bfloat16
HIGHEST = lax.Precision.HIGHEST

HEAD = 64
GRID_W = 64
CONV_K = 3
TOP_K = 2
LN_EPS = 1e-5
GN_EPS = 64e-5
NORM_EPS = 1e-12
PAD_LOGIT = -1e30
MXU_WIDTH_V7X = 256
GROUP = MXU_WIDTH_V7X // HEAD
CHUNK = MXU_WIDTH_V7X // GROUP
TOK_BLOCK = 256
LANE = 128
VMEM_LIMIT_V7X = 56 * 1024 * 1024
FFN_M_BLOCKS_PER_BATCH = 4
FFN_F_BLOCK = 512
MOD_N_BLOCK = 1536


def _mm(a, b):
    return jnp.dot(a.astype(BF16), b.astype(BF16), preferred_element_type=F32)


def _mm_nt(a, b):
    return lax.dot_general(a.astype(BF16), b.astype(BF16), (((1,), (1,)), ((), ())),
                           preferred_element_type=F32)


def _sigmoid(x):
    return 1.0 / (1.0 + jnp.exp(-x))


def _layer_norm(z, g, b):
    mu = jnp.mean(z, axis=-1, keepdims=True)
    d = z - mu
    var = jnp.mean(d * d, axis=-1, keepdims=True)
    return d * lax.rsqrt(var + LN_EPS) * g + b


def _params(sem):
    return pltpu.CompilerParams(dimension_semantics=sem, vmem_limit_bytes=VMEM_LIMIT_V7X)


def _mod_kernel(c_ref, w_ref, b_ref, o_ref):
    c = c_ref[...]
    s = c * _sigmoid(c)
    o_ref[...] = jnp.dot(s, w_ref[...], precision=HIGHEST, preferred_element_type=F32) + b_ref[...]


def _modulation(cc, w_mod, b_mod):
    depth, d, n = w_mod.shape
    tn = MOD_N_BLOCK
    return pl.pallas_call(
        _mod_kernel,
        grid=(depth, n // tn),
        in_specs=[pl.BlockSpec((8, d), lambda l, j: (0, 0)),
                  pl.BlockSpec((None, d, tn), lambda l, j: (l, 0, j)),
                  pl.BlockSpec((None, 1, tn), lambda l, j: (l, 0, j))],
        out_specs=pl.BlockSpec((None, 8, tn), lambda l, j: (l, 0, j)),
        out_shape=jax.ShapeDtypeStruct((depth, 8, n), F32),
        compiler_params=_params(("parallel", "parallel")),
        name="adaln_modulation",
    )(cc, w_mod, b_mod.reshape(depth, 1, n))


def _inproj_kernel(x_ref, mod_ref, w_ref, w2s_ref, a2s_ref, w0_ref, a0_ref, kk_ref, ka_ref, rk_ref, g2_ref,
                   cw_ref, ones_ref, sh_ref, dir_ref, ex_ref, *, width, blocks_per_batch):
    wd_ = width
    x = x_ref[...]
    h = x * (1.0 + mod_ref[1:2, :]) + mod_ref[0:1, :]
    p = _mm(h, w_ref[...])
    r = p[:, 0:wd_]
    k = p[:, wd_:2 * wd_]
    v = p[:, 2 * wd_:3 * wd_]
    o = 3 * wd_
    wd = p[:, o:o + LANE]
    ad = p[:, o + LANE:o + 2 * LANE]
    gd = p[:, o + 2 * LANE:o + 3 * LANE]
    o = o + 3 * LANE
    cb = p[:, o:o + wd_]
    cc = p[:, o + wd_:o + 2 * wd_]
    ch = p[:, o + 2 * wd_:o + 3 * wd_]
    ones = ones_ref[...]

    z = w0_ref[...] + _mm(jnp.tanh(wd), w2s_ref[...])
    softplus_neg = jnp.maximum(-z, 0.0) + jnp.log(1.0 + jnp.exp(-jnp.abs(z)))
    w_log = -softplus_neg - 0.5
    log_decay = -jnp.exp(w_log)
    a_all = _sigmoid(a0_ref[...] + _mm(ad, a2s_ref[...]))

    kk_raw = k * kk_ref[...]
    sq = kk_raw * kk_raw
    sq_hi = sq.astype(BF16)
    sq_lo = (sq - sq_hi.astype(F32)).astype(BF16)
    ss = (jnp.dot(sq_hi, ones, preferred_element_type=F32) + jnp.dot(sq_lo, ones, preferred_element_type=F32))
    kk = kk_raw / jnp.maximum(jnp.sqrt(ss), NORM_EPS)

    sh_ref[:, 0:wd_] = r
    sh_ref[:, wd_:2 * wd_] = kk
    sh_ref[:, 2 * wd_:3 * wd_] = v
    kd_sum = jnp.zeros_like(k)
    for d in range(2):
        a = a_all[:, d * wd_:(d + 1) * wd_]
        kd = k * (1.0 + (a - 1.0) * ka_ref[...])
        kd_sum = kd_sum + kd
        dir_ref[d, :, 0:wd_] = log_decay[:, d * wd_:(d + 1) * wd_]
        dir_ref[d, :, wd_:2 * wd_] = kd
        dir_ref[d, :, 2 * wd_:3 * wd_] = kk * a
    bonus = _mm(r * kd_sum * rk_ref[...], ones) * v
    gate = _mm(_sigmoid(gd), g2_ref[...])

    u = cc * ch
    rows = lax.broadcasted_iota(jnp.int32, u.shape, 0)
    is_ctx = (pl.program_id(0) % blocks_per_batch) == 0
    period_mask = jnp.where(is_ctx, TOK_BLOCK - 1, GRID_W - 1)
    pos = rows & period_mask
    u_prev = jnp.where(pos == 0, 0.0, pltpu.roll(u, 1, 0))
    u_next = jnp.where(pos == period_mask, 0.0, pltpu.roll(u, TOK_BLOCK - 1, 0))
    conv = cb * (u_prev * cw_ref[0:1, :] + u * cw_ref[1:2, :] + u_next * cw_ref[2:3, :])

    ex_ref[:, 0:wd_] = bonus
    ex_ref[:, wd_:2 * wd_] = gate
    ex_ref[:, 2 * wd_:3 * wd_] = conv


def _inproj(x, modtab, w_in, w2s, a2s, w0, a0, k_k, k_a, r_k, g2, conv_w, ones, *, blocks_per_batch):
    ntok, d = x.shape
    width = k_k.shape[-1]
    pcols = w_in.shape[1]
    nb = ntok // TOK_BLOCK
    const = lambda *shape: pl.BlockSpec(shape, lambda i: (0,) * len(shape))
    return pl.pallas_call(
        functools.partial(_inproj_kernel, width=width, blocks_per_batch=blocks_per_batch),
        grid=(nb,),
        in_specs=[pl.BlockSpec((TOK_BLOCK, d), lambda i: (i, 0)),
                  pl.BlockSpec((None, None, 6, d),
                               lambda i: (i // blocks_per_batch, jnp.minimum(i % blocks_per_batch, 1), 0, 0)),
                  const(d, pcols), const(LANE, 2 * width), const(LANE, 2 * width), const(1, 2 * width),
                  const(1, 2 * width), const(1, width), const(1, width), const(1, width), const(LANE, width),
                  const(CONV_K, width), const(width, width)],
        out_specs=[pl.BlockSpec((TOK_BLOCK, 3 * width), lambda i: (i, 0)),
                   pl.BlockSpec((2, TOK_BLOCK, 3 * width), lambda i: (0, i, 0)),
                   pl.BlockSpec((TOK_BLOCK, 3 * width), lambda i: (i, 0))],
        out_shape=[jax.ShapeDtypeStruct((ntok, 3 * width), F32),
                   jax.ShapeDtypeStruct((2, ntok, 3 * width), F32),
                   jax.ShapeDtypeStruct((ntok, 3 * width), F32)],
        compiler_params=_params(("parallel",)),
        name="inproj_prep",
    )(x, modtab, w_in, w2s, a2s, w0, a0, k_k, k_a, r_k, g2, conv_w, ones)


def _scan_masks(rev):
    n = GROUP * CHUNK
    ri = lax.broadcasted_iota(jnp.int32, (n, n), 0)
    ci = lax.broadcasted_iota(jnp.int32, (n, n), 1)
    same_head = (ri // CHUNK) == (ci // CHUNK)
    tr = ri % CHUNK
    tc = ci % CHUNK
    before = (tc > tr) if rev else (tc < tr)
    strict = same_head & before
    incl = same_head & (before | (tc == tr))
    eye = ri == ci
    levels = []
    s = 1
    while s < CHUNK:
        levels.append(((tr // (2 * s)) == (tc // (2 * s))) & ((tr // s) != (tc // s)))
        s *= 2
    return same_head, strict, incl, eye, levels


def _scan_chunks(streams):
    n = GROUP * CHUNK
    ti = lax.broadcasted_iota(jnp.int32, (CHUNK, CHUNK), 0)
    si = lax.broadcasted_iota(jnp.int32, (CHUNK, CHUNK), 1)

    def prepare(r, kk, v, lw, kd, b, s_prev, masks, rev):
        same_head = masks[0]
        tri = jnp.where((si >= ti) if rev else (si <= ti), 1.0, 0.0).astype(F32)
        g_inc = jnp.dot(tri, lw, precision=HIGHEST, preferred_element_type=F32)
        g_exc = g_inc - lw
        g_end = g_inc[0:1, :] if rev else g_inc[CHUNK - 1:CHUNK, :]
        e_neg = jnp.exp(-g_inc)
        e_end = jnp.exp(g_end - g_inc)

        def blk(m):
            return jnp.where(same_head, jnp.concatenate([m] * GROUP, axis=0), 0.0)

        v_f = blk(v)
        return dict(kk=blk(kk * jnp.exp(g_exc)).astype(BF16), r=blk(r * jnp.exp(g_inc)).astype(BF16),
                    bt=blk(b * e_neg).astype(BF16), kt=blk(kd * e_neg).astype(BF16),
                    bend=blk(b * e_end).astype(BF16), kend=blk(kd * e_end).astype(BF16),
                    v=v_f.astype(BF16), vt=v_f.T.astype(BF16), decay=jnp.exp(g_end),
                    s=s_prev, s_b=s_prev.astype(BF16))

    ops = [prepare(*st) for st in streams]
    masks = [st[7] for st in streams]
    pairs = [_mm_nt(jnp.concatenate([o["kk"], o["r"]], axis=0), jnp.concatenate([o["bt"], o["kt"]], axis=0))
             for o in ops]
    l_b = [jnp.where(m[1], p[0:n, 0:n], 0.0) for p, m in zip(pairs, masks)]
    l_k = [jnp.where(m[1], p[0:n, n:2 * n], 0.0).astype(BF16) for p, m in zip(pairs, masks)]
    a_r = [jnp.concatenate([jnp.where(m[2], p[n:2 * n, 0:n], 0.0).astype(BF16),
                            jnp.where(m[2], p[n:2 * n, n:2 * n], 0.0).astype(BF16)], axis=1)
           for p, m in zip(pairs, masks)]

    inv = [jnp.where(m[3], 1.0, 0.0) - jnp.where(m[4][0], lb, 0.0) for lb, m in zip(l_b, masks)]
    for lvl in range(1, len(masks[0][4])):
        inv_b = [a.astype(BF16) for a in inv]
        half = [_mm(jnp.where(m[4][lvl], lb, 0.0), ib) for lb, ib, m in zip(l_b, inv_b, masks)]
        inv = [a - _mm(ib, h) for a, ib, h in zip(inv, inv_b, half)]

    x = [-(_mm_nt(o["kk"], o["s_b"]) + _mm(lk, o["v"])) for o, lk in zip(ops, l_k)]
    u = [_mm(a, xx) for a, xx in zip(inv, x)]
    y_b = [_mm_nt(o["r"], o["s_b"]) + _mm(ar, jnp.concatenate([uu.astype(BF16), o["v"]], axis=0))
           for o, ar, uu in zip(ops, a_r, u)]
    s_new = [o["decay"] * o["s"] + _mm(jnp.concatenate([uu.T.astype(BF16), o["vt"]], axis=1),
                                      jnp.concatenate([o["bend"], o["kend"]], axis=0))
             for o, uu in zip(ops, u)]
    ys = []
    for yb in y_b:
        y = yb[0:CHUNK]
        for hh in range(1, GROUP):
            y = y + yb[hh * CHUNK:(hh + 1) * CHUNK]
        ys.append(y)
    return ys, s_new


def _scan_kernel(shf_ref, shb_ref, df_ref, db_ref, yf_ref, yb_ref, s_ref, *, width):
    @pl.when(pl.program_id(1) == 0)
    def _():
        s_ref[...] = jnp.zeros_like(s_ref)

    gw = GROUP * HEAD
    n_groups = width // gw
    streams, sinks = [], []
    for d, (sh, dr, yo) in enumerate(((shf_ref, df_ref, yf_ref), (shb_ref, db_ref, yb_ref))):
        rev = d == 1
        masks = _scan_masks(rev)
        for g in range(n_groups):
            lo = g * gw
            streams.append((sh[:, lo:lo + gw], sh[:, width + lo:width + lo + gw],
                            sh[:, 2 * width + lo:2 * width + lo + gw],
                            dr[:, lo:lo + gw], dr[:, width + lo:width + lo + gw],
                            dr[:, 2 * width + lo:2 * width + lo + gw],
                            s_ref[d * n_groups + g], masks, rev))
            sinks.append((yo, lo, d * n_groups + g))
    ys, s_new = _scan_chunks(streams)
    for (yo, lo, idx), y, s in zip(sinks, ys, s_new):
        yo[:, lo:lo + gw] = y
        s_ref[idx] = s


def _wkv_scan(shared, per_dir, *, batch, ctx_chunks):
    ntok, w3 = shared.shape
    width = w3 // 3
    nc = ntok // batch // CHUNK
    n_groups = width // (GROUP * HEAD)

    def fwd(b, i):
        return b * nc + i

    def bwd(b, i):
        return b * nc + jnp.where(i < ctx_chunks, ctx_chunks - 1 - i, nc - 1 - (i - ctx_chunks))

    return pl.pallas_call(
        functools.partial(_scan_kernel, width=width),
        grid=(batch, nc),
        in_specs=[pl.BlockSpec((CHUNK, w3), lambda b, i: (fwd(b, i), 0)),
                  pl.BlockSpec((CHUNK, w3), lambda b, i: (bwd(b, i), 0)),
                  pl.BlockSpec((None, CHUNK, w3), lambda b, i: (0, fwd(b, i), 0)),
                  pl.BlockSpec((None, CHUNK, w3), lambda b, i: (1, bwd(b, i), 0))],
        out_specs=[pl.BlockSpec((CHUNK, width), lambda b, i: (fwd(b, i), 0)),
                   pl.BlockSpec((CHUNK, width), lambda b, i: (bwd(b, i), 0))],
        out_shape=[jax.ShapeDtypeStruct((ntok, width), F32)] * 2,
        scratch_shapes=[pltpu.VMEM((2 * n_groups, GROUP * HEAD, GROUP * HEAD), F32)],
        compiler_params=_params(("parallel", "arbitrary")),
        name="wkv_scan",
    )(shared, shared, per_dir, per_dir)


def _top2_gates(logits):
    lane = lax.broadcasted_iota(jnp.int32, logits.shape, 1)
    n = logits.shape[-1]
    m1 = jnp.max(logits, axis=-1, keepdims=True)
    i1 = jnp.min(jnp.where(logits == m1, lane, n), axis=-1, keepdims=True)
    rest = jnp.where(lane == i1, PAD_LOGIT, logits)
    m2 = jnp.max(rest, axis=-1, keepdims=True)
    i2 = jnp.min(jnp.where(rest == m2, lane, n), axis=-1, keepdims=True)
    e = jnp.exp(m2 - m1)
    return jnp.where(lane == i1, 1.0 / (1.0 + e), 0.0) + jnp.where(lane == i2, e / (1.0 + e), 0.0)


def _outproj_kernel(*refs, width, alpha, moe):
    if moe:
        (yf_ref, yb_ref, ex_ref, x_ref, mod_ref, lg_ref, lb_ref, wo_ref, g1_ref, b1_ref, ones_ref,
         rw_ref, rb_ref, x1_ref, h2_ref, gate_ref) = refs
    else:
        (yf_ref, yb_ref, ex_ref, x_ref, mod_ref, lg_ref, lb_ref, wo_ref, g1_ref, b1_ref, ones_ref,
         x1_ref, h2_ref) = refs
    ones = ones_ref[...]
    y = yf_ref[...] + yb_ref[...]
    inv_n = 1.0 / HEAD
    mu = _mm(y, ones) * inv_n
    dlt = y - mu
    var = _mm(dlt * dlt, ones) * inv_n
    yn = dlt * lax.rsqrt(var + GN_EPS)
    bonus = ex_ref[:, 0:width]
    gate = ex_ref[:, width:2 * width]
    conv = ex_ref[:, 2 * width:3 * width]
    rwkv = (yn * lg_ref[...] + lb_ref[...] + bonus) * gate
    cat = jnp.concatenate([rwkv.astype(BF16), conv.astype(BF16)], axis=1)
    mix = jnp.dot(cat, wo_ref[...], preferred_element_type=F32)
    x1 = _layer_norm(alpha * x_ref[...] + mod_ref[2:3, :] * mix, g1_ref[...], b1_ref[...])
    h2 = x1 * (1.0 + mod_ref[4:5, :]) + mod_ref[3:4, :]
    x1_ref[...] = x1
    h2_ref[...] = h2.astype(BF16)
    if moe:
        logits = jnp.dot(h2, rw_ref[...], precision=HIGHEST, preferred_element_type=F32) + rb_ref[...]
        gate_ref[...] = _top2_gates(logits)


def _outproj(yf, yb, extras, x, modtab, lnx_g, lnx_b, w_out, ln_g, ln_b, ones, router, *, alpha,
             blocks_per_batch):
    ntok, d = x.shape
    width = yf.shape[1]
    nb = ntok // TOK_BLOCK
    moe = router is not None
    const = lambda *shape: pl.BlockSpec(shape, lambda i: (0,) * len(shape))
    row = lambda cols: pl.BlockSpec((TOK_BLOCK, cols), lambda i: (i, 0))
    in_specs = [row(width), row(width), row(3 * width), row(d),
                pl.BlockSpec((None, None, 6, d),
                             lambda i: (i // blocks_per_batch, jnp.minimum(i % blocks_per_batch, 1), 0, 0)),
                const(1, width), const(1, width), const(2 * width, d), const(1, d), const(1, d),
                const(width, width)]
    out_specs = [row(d), row(d)]
    out_shape = [jax.ShapeDtypeStruct((ntok, d), F32), jax.ShapeDtypeStruct((ntok, d), BF16)]
    args = [yf, yb, extras, x, modtab, lnx_g, lnx_b, w_out, ln_g, ln_b, ones]
    if moe:
        in_specs += [const(d, LANE), const(1, LANE)]
        out_specs.append(row(LANE))
        out_shape.append(jax.ShapeDtypeStruct((ntok, LANE), F32))
        args += list(router)
    return pl.pallas_call(
        functools.partial(_outproj_kernel, width=width, alpha=alpha, moe=moe),
        grid=(nb,), in_specs=in_specs, out_specs=out_specs, out_shape=out_shape,
        compiler_params=_params(("parallel",)),
        name="readout_outproj",
    )(*args)


def _ffn_kernel(*refs, alpha, moe, ctx_len, m_blocks_per_batch):
    if moe:
        h_ref, x1_ref, gates_ref, w1_ref, w3_ref, w2_ref, mod_ref, g_ref, b_ref, o_ref, acc_ref = refs
    else:
        h_ref, x1_ref, w1_ref, w3_ref, w2_ref, mod_ref, g_ref, b_ref, o_ref, acc_ref = refs
    e = pl.program_id(1)
    f = pl.program_id(2)

    @pl.when((e == 0) & (f == 0))
    def _():
        acc_ref[...] = jnp.zeros_like(acc_ref)

    h = h_ref[...]
    a = jnp.dot(h, w1_ref[...], preferred_element_type=F32)
    b = jnp.dot(h, w3_ref[...], preferred_element_type=F32)
    s = a * _sigmoid(a) * b
    if moe:
        lane = lax.broadcasted_iota(jnp.int32, gates_ref.shape, 1)
        s = s * jnp.sum(jnp.where(lane == e, gates_ref[...], 0.0), axis=-1, keepdims=True)
    acc_ref[...] += jnp.dot(s.astype(BF16), w2_ref[...], preferred_element_type=F32)

    @pl.when((e == pl.num_programs(1) - 1) & (f == pl.num_programs(2) - 1))
    def _():
        tm = acc_ref.shape[0]
        rows = lax.broadcasted_iota(jnp.int32, acc_ref.shape, 0)
        row_in_batch = rows + (pl.program_id(0) % m_blocks_per_batch) * tm
        gate2 = jnp.where(row_in_batch < ctx_len, mod_ref[0, 5:6, :], mod_ref[1, 5:6, :])
        o_ref[...] = _layer_norm(alpha * x1_ref[...] + gate2 * acc_ref[...], g_ref[...], b_ref[...])


def _ffn(h2, x1, gates, w1, w3, w2, modtab, ln_g, ln_b, *, alpha, ctx_len, batch):
    ntok, d = x1.shape
    n_exp, _, dff = w1.shape
    mpb = FFN_M_BLOCKS_PER_BATCH
    tm = ntok // batch // mpb
    tf = min(FFN_F_BLOCK, dff)
    moe = gates is not None
    row = lambda cols: pl.BlockSpec((tm, cols), lambda m, e, f: (m, 0))
    const = lambda *shape: pl.BlockSpec(shape, lambda m, e, f: (0,) * len(shape))
    in_specs = [row(d), row(d)] + ([row(LANE)] if moe else []) + [
        pl.BlockSpec((None, d, tf), lambda m, e, f: (e, 0, f)),
        pl.BlockSpec((None, d, tf), lambda m, e, f: (e, 0, f)),
        pl.BlockSpec((None, tf, d), lambda m, e, f: (e, f, 0)),
        pl.BlockSpec((None, 2, 6, d), lambda m, e, f: (m // mpb, 0, 0, 0)),
        const(1, d), const(1, d)]
    args = [h2, x1] + ([gates] if moe else []) + [w1, w3, w2, modtab, ln_g, ln_b]
    return pl.pallas_call(
        functools.partial(_ffn_kernel, alpha=alpha, moe=moe, ctx_len=ctx_len, m_blocks_per_batch=mpb),
        grid=(ntok // tm, n_exp, dff // tf),
        in_specs=in_specs, out_specs=row(d),
        out_shape=jax.ShapeDtypeStruct((ntok, d), F32),
        scratch_shapes=[pltpu.VMEM((tm, d), F32)],
        compiler_params=_params(("parallel", "arbitrary", "arbitrary")),
        name="moe_swiglu" if moe else "dense_swiglu",
    )(*args)


def _head_ones(width):
    hid = jnp.arange(width) // HEAD
    return (hid[:, None] == hid[None, :]).astype(BF16)


def _stack_lora(w):
    _, rank, width = w.shape
    z = jnp.zeros((rank, width), w.dtype)
    return jnp.concatenate([jnp.concatenate([w[0], z], axis=1), jnp.concatenate([z, w[1]], axis=1)], axis=0)


def kernel(x, c, ctx, c_ctx, w_mod, b_mod, w_in, rk_w0, rk_w2, rk_a0, rk_a2, rk_kk, rk_ka, rk_rk, rk_g2,
           rk_lnx_g, rk_lnx_b, conv_w, w_out, ln1_g, ln1_b, ln2_g, ln2_b, ffn_w1, ffn_w3, ffn_w2,
           router_w, router_b, moe_w1, moe_w3, moe_w2):
    batch, seq, d = x.shape
    ctx_len = ctx.shape[1]
    depth = w_in.shape[0]
    width = rk_kk.shape[-1]
    n_exp = router_w.shape[-1]
    tok = ctx_len + seq
    assert ctx_len == TOK_BLOCK and seq % TOK_BLOCK == 0 and seq % GRID_W == 0 and batch < 8
    assert 2 * rk_w2.shape[2] == LANE and 2 * rk_a2.shape[2] == LANE and rk_g2.shape[1] == LANE
    assert width % (GROUP * HEAD) == 0 and n_exp <= LANE and tok % (FFN_M_BLOCKS_PER_BATCH * 16) == 0
    blocks_per_batch = tok // TOK_BLOCK
    alpha = (2 * depth) ** 0.25

    cc = jnp.concatenate([c, c_ctx[None, :], jnp.zeros((8 - batch - 1, d), F32)], axis=0)
    mods = _modulation(cc, w_mod, b_mod).reshape(depth, 8, 6, d)
    modtab = jnp.stack([jnp.broadcast_to(mods[:, batch][:, None], (depth, batch, 6, d)), mods[:, :batch]],
                       axis=2)

    xs = jnp.concatenate([ctx, x], axis=1).reshape(batch * tok, d)
    ones = _head_ones(width)
    for l in range(depth):
        shared, per_dir, extras = _inproj(
            xs, modtab[l], w_in[l].astype(BF16), _stack_lora(rk_w2[l]).astype(BF16),
            _stack_lora(rk_a2[l]).astype(BF16), rk_w0[l].reshape(1, 2 * width), rk_a0[l].reshape(1, 2 * width),
            rk_kk[l][None], rk_ka[l][None], rk_rk[l].reshape(1, width), rk_g2[l].astype(BF16), conv_w[l], ones,
            blocks_per_batch=blocks_per_batch)
        yf, yb = _wkv_scan(shared, per_dir, batch=batch, ctx_chunks=ctx_len // CHUNK)
        i = l // 2
        router = None
        if l % 2 == 1:
            router = (jnp.pad(router_w[i], ((0, 0), (0, LANE - n_exp))),
                      jnp.pad(router_b[i], (0, LANE - n_exp), constant_values=PAD_LOGIT)[None])
        res = _outproj(yf, yb, extras, xs, modtab[l], rk_lnx_g[l][None], rk_lnx_b[l][None],
                       w_out[l].astype(BF16), ln1_g[l][None], ln1_b[l][None], ones, router,
                       alpha=alpha, blocks_per_batch=blocks_per_batch)
        if l % 2 == 0:
            x1, h2 = res
            gates = None
            w1, w3, w2 = ffn_w1[i][None], ffn_w3[i][None], ffn_w2[i][None]
        else:
            x1, h2, gates = res
            w1, w3, w2 = moe_w1[i], moe_w3[i], moe_w2[i]
        xs = _ffn(h2, x1, gates, w1.astype(BF16), w3.astype(BF16), w2.astype(BF16), modtab[l],
                  ln2_g[l][None], ln2_b[l][None], alpha=alpha, ctx_len=ctx_len, batch=batch)
    return xs.reshape(batch, tok, d)[:, ctx_len:]
```

```python
import functools

import jax
import jax.numpy as jnp
from jax import lax
from jax.experimental import pallas as pl
from jax.experimental.pallas import tpu as pltpu

F32 = jnp.float32
BF16 = jnp.bfloat16
HIGHEST = lax.Precision.HIGHEST

HEAD = 64
GRID_W = 64
CONV_K = 3
TOP_K = 2
LN_EPS = 1e-5
GN_EPS = 64e-5
NORM_EPS = 1e-12
PAD_LOGIT = -1e30
ROUTE_E1, ROUTE_E2, ROUTE_W1, ROUTE_W2 = 0, 1, 2, 3
MOE_ROW_TILE = 512
DMA_LOOP_UNROLL = 8
MXU_WIDTH_V7X = 256
GROUP = MXU_WIDTH_V7X // HEAD
CHUNK = MXU_WIDTH_V7X // GROUP
TOK_BLOCK = 256
LANE = 128
VMEM_LIMIT_V7X = 56 * 1024 * 1024
FFN_M_BLOCKS_PER_BATCH = 4
FFN_F_BLOCK = 512
MOD_N_BLOCK = 1536


def _mm(a, b):
    return jnp.dot(a.astype(BF16), b.astype(BF16), preferred_element_type=F32)


def _mm_nt(a, b):
    return lax.dot_general(a.astype(BF16), b.astype(BF16), (((1,), (1,)), ((), ())),
                           preferred_element_type=F32)


def _sigmoid(x):
    return 1.0 / (1.0 + jnp.exp(-x))


def _layer_norm(z, g, b):
    mu = jnp.mean(z, axis=-1, keepdims=True)
    d = z - mu
    var = jnp.mean(d * d, axis=-1, keepdims=True)
    return d * lax.rsqrt(var + LN_EPS) * g + b


def _params(sem):
    return pltpu.CompilerParams(dimension_semantics=sem, vmem_limit_bytes=VMEM_LIMIT_V7X)


def _mod_kernel(c_ref, w_ref, b_ref, o_ref):
    c = c_ref[...]
    s = c * _sigmoid(c)
    o_ref[...] = jnp.dot(s, w_ref[...], precision=HIGHEST, preferred_element_type=F32) + b_ref[...]


def _modulation(cc, w_mod, b_mod):
    depth, d, n = w_mod.shape
    tn = MOD_N_BLOCK
    return pl.pallas_call(
        _mod_kernel,
        grid=(depth, n // tn),
        in_specs=[pl.BlockSpec((8, d), lambda l, j: (0, 0)),
                  pl.BlockSpec((None, d, tn), lambda l, j: (l, 0, j)),
                  pl.BlockSpec((None, 1, tn), lambda l, j: (l, 0, j))],
        out_specs=pl.BlockSpec((None, 8, tn), lambda l, j: (l, 0, j)),
        out_shape=jax.ShapeDtypeStruct((depth, 8, n), F32),
        compiler_params=_params(("parallel", "parallel")),
        name="adaln_modulation",
    )(cc, w_mod, b_mod.reshape(depth, 1, n))


def _inproj_kernel(x_ref, mod_ref, w_ref, w2s_ref, a2s_ref, w0_ref, a0_ref, kk_ref, ka_ref, rk_ref, g2_ref,
                   cw_ref, ones_ref, sh_ref, dir_ref, ex_ref, *, width, blocks_per_batch):
    wd_ = width
    x = x_ref[...]
    h = x * (1.0 + mod_ref[1:2, :]) + mod_ref[0:1, :]
    p = _mm(h, w_ref[...])
    r = p[:, 0:wd_]
    k = p[:, wd_:2 * wd_]
    v = p[:, 2 * wd_:3 * wd_]
    o = 3 * wd_
    wd = p[:, o:o + LANE]
    ad = p[:, o + LANE:o + 2 * LANE]
    gd = p[:, o + 2 * LANE:o + 3 * LANE]
    o = o + 3 * LANE
    cb = p[:, o:o + wd_]
    cc = p[:, o + wd_:o + 2 * wd_]
    ch = p[:, o + 2 * wd_:o + 3 * wd_]
    ones = ones_ref[...]

    z = w0_ref[...] + _mm(jnp.tanh(wd), w2s_ref[...])
    softplus_neg = jnp.maximum(-z, 0.0) + jnp.log(1.0 + jnp.exp(-jnp.abs(z)))
    w_log = -softplus_neg - 0.5
    log_decay = -jnp.exp(w_log)
    a_all = _sigmoid(a0_ref[...] + _mm(ad, a2s_ref[...]))

    kk_raw = k * kk_ref[...]
    sq = kk_raw * kk_raw
    sq_hi = sq.astype(BF16)
    sq_lo = (sq - sq_hi.astype(F32)).astype(BF16)
    ss = (jnp.dot(sq_hi, ones, preferred_element_type=F32) + jnp.dot(sq_lo, ones, preferred_element_type=F32))
    kk = kk_raw / jnp.maximum(jnp.sqrt(ss), NORM_EPS)

    sh_ref[:, 0:wd_] = r
    sh_ref[:, wd_:2 * wd_] = kk
    sh_ref[:, 2 * wd_:3 * wd_] = v
    kd_sum = jnp.zeros_like(k)
    for d in range(2):
        a = a_all[:, d * wd_:(d + 1) * wd_]
        kd = k * (1.0 + (a - 1.0) * ka_ref[...])
        kd_sum = kd_sum + kd
        dir_ref[d, :, 0:wd_] = log_decay[:, d * wd_:(d + 1) * wd_]
        dir_ref[d, :, wd_:2 * wd_] = kd
        dir_ref[d, :, 2 * wd_:3 * wd_] = kk * a
    bonus = _mm(r * kd_sum * rk_ref[...], ones) * v
    gate = _mm(_sigmoid(gd), g2_ref[...])

    u = cc * ch
    rows = lax.broadcasted_iota(jnp.int32, u.shape, 0)
    is_ctx = (pl.program_id(0) % blocks_per_batch) == 0
    period_mask = jnp.where(is_ctx, TOK_BLOCK - 1, GRID_W - 1)
    pos = rows & period_mask
    u_prev = jnp.where(pos == 0, 0.0, pltpu.roll(u, 1, 0))
    u_next = jnp.where(pos == period_mask, 0.0, pltpu.roll(u, TOK_BLOCK - 1, 0))
    conv = cb * (u_prev * cw_ref[0:1, :] + u * cw_ref[1:2, :] + u_next * cw_ref[2:3, :])

    ex_ref[:, 0:wd_] = bonus
    ex_ref[:, wd_:2 * wd_] = gate
    ex_ref[:, 2 * wd_:3 * wd_] = conv


def _inproj(x, modtab, w_in, w2s, a2s, w0, a0, k_k, k_a, r_k, g2, conv_w, ones, *, blocks_per_batch):
    ntok, d = x.shape
    width = k_k.shape[-1]
    pcols = w_in.shape[1]
    nb = ntok // TOK_BLOCK
    const = lambda *shape: pl.BlockSpec(shape, lambda i: (0,) * len(shape))
    return pl.pallas_call(
        functools.partial(_inproj_kernel, width=width, blocks_per_batch=blocks_per_batch),
        grid=(nb,),
        in_specs=[pl.BlockSpec((TOK_BLOCK, d), lambda i: (i, 0)),
                  pl.BlockSpec((None, None, 6, d),
                               lambda i: (i // blocks_per_batch, jnp.minimum(i % blocks_per_batch, 1), 0, 0)),
                  const(d, pcols), const(LANE, 2 * width), const(LANE, 2 * width), const(1, 2 * width),
                  const(1, 2 * width), const(1, width), const(1, width), const(1, width), const(LANE, width),
                  const(CONV_K, width), const(width, width)],
        out_specs=[pl.BlockSpec((TOK_BLOCK, 3 * width), lambda i: (i, 0)),
                   pl.BlockSpec((2, TOK_BLOCK, 3 * width), lambda i: (0, i, 0)),
                   pl.BlockSpec((TOK_BLOCK, 3 * width), lambda i: (i, 0))],
        out_shape=[jax.ShapeDtypeStruct((ntok, 3 * width), F32),
                   jax.ShapeDtypeStruct((2, ntok, 3 * width), F32),
                   jax.ShapeDtypeStruct((ntok, 3 * width), F32)],
        compiler_params=_params(("parallel",)),
        name="inproj_prep",
    )(x, modtab, w_in, w2s, a2s, w0, a0, k_k, k_a, r_k, g2, conv_w, ones)


def _scan_masks(rev):
    n = GROUP * CHUNK
    ri = lax.broadcasted_iota(jnp.int32, (n, n), 0)
    ci = lax.broadcasted_iota(jnp.int32, (n, n), 1)
    same_head = (ri // CHUNK) == (ci // CHUNK)
    tr = ri % CHUNK
    tc = ci % CHUNK
    before = (tc > tr) if rev else (tc < tr)
    strict = same_head & before
    incl = same_head & (before | (tc == tr))
    eye = ri == ci
    levels = []
    s = 1
    while s < CHUNK:
        levels.append(((tr // (2 * s)) == (tc // (2 * s))) & ((tr // s) != (tc // s)))
        s *= 2
    return same_head, strict, incl, eye, levels


def _scan_chunks(streams):
    n = GROUP * CHUNK
    ti = lax.broadcasted_iota(jnp.int32, (CHUNK, CHUNK), 0)
    si = lax.broadcasted_iota(jnp.int32, (CHUNK, CHUNK), 1)

    def prepare(r, kk, v, lw, kd, b, s_prev, masks, rev):
        same_head = masks[0]
        tri = jnp.where((si >= ti) if rev else (si <= ti), 1.0, 0.0).astype(F32)
        g_inc = jnp.dot(tri, lw, precision=HIGHEST, preferred_element_type=F32)
        g_exc = g_inc - lw
        g_end = g_inc[0:1, :] if rev else g_inc[CHUNK - 1:CHUNK, :]
        e_neg = jnp.exp(-g_inc)
        e_end = jnp.exp(g_end - g_inc)

        def blk(m):
            return jnp.where(same_head, jnp.concatenate([m] * GROUP, axis=0), 0.0)

        v_f = blk(v)
        return dict(kk=blk(kk * jnp.exp(g_exc)).astype(BF16), r=blk(r * jnp.exp(g_inc)).astype(BF16),
                    bt=blk(b * e_neg).astype(BF16), kt=blk(kd * e_neg).astype(BF16),
                    bend=blk(b * e_end).astype(BF16), kend=blk(kd * e_end).astype(BF16),
                    v=v_f.astype(BF16), vt=v_f.T.astype(BF16), decay=jnp.exp(g_end),
                    s=s_prev, s_b=s_prev.astype(BF16))

    ops = [prepare(*st) for st in streams]
    masks = [st[7] for st in streams]
    pairs = [_mm_nt(jnp.concatenate([o["kk"], o["r"]], axis=0), jnp.concatenate([o["bt"], o["kt"]], axis=0))
             for o in ops]
    l_b = [jnp.where(m[1], p[0:n, 0:n], 0.0) for p, m in zip(pairs, masks)]
    l_k = [jnp.where(m[1], p[0:n, n:2 * n], 0.0).astype(BF16) for p, m in zip(pairs, masks)]
    a_r = [jnp.concatenate([jnp.where(m[2], p[n:2 * n, 0:n], 0.0).astype(BF16),
                            jnp.where(m[2], p[n:2 * n, n:2 * n], 0.0).astype(BF16)], axis=1)
           for p, m in zip(pairs, masks)]

    inv = [jnp.where(m[3], 1.0, 0.0) - jnp.where(m[4][0], lb, 0.0) for lb, m in zip(l_b, masks)]
    for lvl in range(1, len(masks[0][4])):
        inv_b = [a.astype(BF16) for a in inv]
        half = [_mm(jnp.where(m[4][lvl], lb, 0.0), ib) for lb, ib, m in zip(l_b, inv_b, masks)]
        inv = [a - _mm(ib, h) for a, ib, h in zip(inv, inv_b, half)]

    x = [-(_mm_nt(o["kk"], o["s_b"]) + _mm(lk, o["v"])) for o, lk in zip(ops, l_k)]
    u = [_mm(a, xx) for a, xx in zip(inv, x)]
    y_b = [_mm_nt(o["r"], o["s_b"]) + _mm(ar, jnp.concatenate([uu.astype(BF16), o["v"]], axis=0))
           for o, ar, uu in zip(ops, a_r, u)]
    s_new = [o["decay"] * o["s"] + _mm(jnp.concatenate([uu.T.astype(BF16), o["vt"]], axis=1),
                                      jnp.concatenate([o["bend"], o["kend"]], axis=0))
             for o, uu in zip(ops, u)]
    ys = []
    for yb in y_b:
        y = yb[0:CHUNK]
        for hh in range(1, GROUP):
            y = y + yb[hh * CHUNK:(hh + 1) * CHUNK]
        ys.append(y)
    return ys, s_new


def _scan_kernel(shf_ref, shb_ref, df_ref, db_ref, yf_ref, yb_ref, s_ref, *, width):
    @pl.when(pl.program_id(1) == 0)
    def _():
        s_ref[...] = jnp.zeros_like(s_ref)

    gw = GROUP * HEAD
    n_groups = width // gw
    streams, sinks = [], []
    for d, (sh, dr, yo) in enumerate(((shf_ref, df_ref, yf_ref), (shb_ref, db_ref, yb_ref))):
        rev = d == 1
        masks = _scan_masks(rev)
        for g in range(n_groups):
            lo = g * gw
            streams.append((sh[:, lo:lo + gw], sh[:, width + lo:width + lo + gw],
                            sh[:, 2 * width + lo:2 * width + lo + gw],
                            dr[:, lo:lo + gw], dr[:, width + lo:width + lo + gw],
                            dr[:, 2 * width + lo:2 * width + lo + gw],
                            s_ref[d * n_groups + g], masks, rev))
            sinks.append((yo, lo, d * n_groups + g))
    ys, s_new = _scan_chunks(streams)
    for (yo, lo, idx), y, s in zip(sinks, ys, s_new):
        yo[:, lo:lo + gw] = y
        s_ref[idx] = s


def _wkv_scan(shared, per_dir, *, batch, ctx_chunks):
    ntok, w3 = shared.shape
    width = w3 // 3
    nc = ntok // batch // CHUNK
    n_groups = width // (GROUP * HEAD)

    def fwd(b, i):
        return b * nc + i

    def bwd(b, i):
        return b * nc + jnp.where(i < ctx_chunks, ctx_chunks - 1 - i, nc - 1 - (i - ctx_chunks))

    return pl.pallas_call(
        functools.partial(_scan_kernel, width=width),
        grid=(batch, nc),
        in_specs=[pl.BlockSpec((CHUNK, w3), lambda b, i: (fwd(b, i), 0)),
                  pl.BlockSpec((CHUNK, w3), lambda b, i: (bwd(b, i), 0)),
                  pl.BlockSpec((None, CHUNK, w3), lambda b, i: (0, fwd(b, i), 0)),
                  pl.BlockSpec((None, CHUNK, w3), lambda b, i: (1, bwd(b, i), 0))],
        out_specs=[pl.BlockSpec((CHUNK, width), lambda b, i: (fwd(b, i), 0)),
                   pl.BlockSpec((CHUNK, width), lambda b, i: (bwd(b, i), 0))],
        out_shape=[jax.ShapeDtypeStruct((ntok, width), F32)] * 2,
        scratch_shapes=[pltpu.VMEM((2 * n_groups, GROUP * HEAD, GROUP * HEAD), F32)],
        compiler_params=_params(("parallel", "arbitrary")),
        name="wkv_scan",
    )(shared, shared, per_dir, per_dir)


def _top2_route(logits):
    lane = lax.broadcasted_iota(jnp.int32, logits.shape, 1)
    n = logits.shape[-1]
    m1 = jnp.max(logits, axis=-1, keepdims=True)
    i1 = jnp.min(jnp.where(logits == m1, lane, n), axis=-1, keepdims=True)
    rest = jnp.where(lane == i1, PAD_LOGIT, logits)
    m2 = jnp.max(rest, axis=-1, keepdims=True)
    i2 = jnp.min(jnp.where(rest == m2, lane, n), axis=-1, keepdims=True)
    e = jnp.exp(m2 - m1)
    rec = jnp.where(lane == ROUTE_W2, e / (1.0 + e), 0.0)
    rec = jnp.where(lane == ROUTE_W1, 1.0 / (1.0 + e), rec)
    rec = jnp.where(lane == ROUTE_E2, i2.astype(F32), rec)
    return jnp.where(lane == ROUTE_E1, i1.astype(F32), rec)


def _outproj_kernel(*refs, width, alpha, moe):
    if moe:
        (yf_ref, yb_ref, ex_ref, x_ref, mod_ref, lg_ref, lb_ref, wo_ref, g1_ref, b1_ref, ones_ref,
         rw_ref, rb_ref, x1_ref, h2_ref, gate_ref) = refs
    else:
        (yf_ref, yb_ref, ex_ref, x_ref, mod_ref, lg_ref, lb_ref, wo_ref, g1_ref, b1_ref, ones_ref,
         x1_ref, h2_ref) = refs
    ones = ones_ref[...]
    y = yf_ref[...] + yb_ref[...]
    inv_n = 1.0 / HEAD
    mu = _mm(y, ones) * inv_n
    dlt = y - mu
    var = _mm(dlt * dlt, ones) * inv_n
    yn = dlt * lax.rsqrt(var + GN_EPS)
    bonus = ex_ref[:, 0:width]
    gate = ex_ref[:, width:2 * width]
    conv = ex_ref[:, 2 * width:3 * width]
    rwkv = (yn * lg_ref[...] + lb_ref[...] + bonus) * gate
    cat = jnp.concatenate([rwkv.astype(BF16), conv.astype(BF16)], axis=1)
    mix = jnp.dot(cat, wo_ref[...], preferred_element_type=F32)
    x1 = _layer_norm(alpha * x_ref[...] + mod_ref[2:3, :] * mix, g1_ref[...], b1_ref[...])
    h2 = x1 * (1.0 + mod_ref[4:5, :]) + mod_ref[3:4, :]
    x1_ref[...] = x1
    h2_ref[...] = h2.astype(h2_ref.dtype)
    if moe:
        logits = jnp.dot(h2, rw_ref[...], precision=HIGHEST, preferred_element_type=F32) + rb_ref[...]
        gate_ref[...] = _top2_route(logits)


def _outproj(yf, yb, extras, x, modtab, lnx_g, lnx_b, w_out, ln_g, ln_b, ones, router, *, alpha,
             blocks_per_batch):
    ntok, d = x.shape
    width = yf.shape[1]
    nb = ntok // TOK_BLOCK
    moe = router is not None
    const = lambda *shape: pl.BlockSpec(shape, lambda i: (0,) * len(shape))
    row = lambda cols: pl.BlockSpec((TOK_BLOCK, cols), lambda i: (i, 0))
    in_specs = [row(width), row(width), row(3 * width), row(d),
                pl.BlockSpec((None, None, 6, d),
                             lambda i: (i // blocks_per_batch, jnp.minimum(i % blocks_per_batch, 1), 0, 0)),
                const(1, width), const(1, width), const(2 * width, d), const(1, d), const(1, d),
                const(width, width)]
    out_specs = [row(d), row(d)]
    out_shape = [jax.ShapeDtypeStruct((ntok, d), F32), jax.ShapeDtypeStruct((ntok, d), F32 if moe else BF16)]
    args = [yf, yb, extras, x, modtab, lnx_g, lnx_b, w_out, ln_g, ln_b, ones]
    if moe:
        in_specs += [const(d, LANE), const(1, LANE)]
        out_specs.append(row(LANE))
        out_shape.append(jax.ShapeDtypeStruct((ntok, LANE), F32))
        args += list(router)
    return pl.pallas_call(
        functools.partial(_outproj_kernel, width=width, alpha=alpha, moe=moe),
        grid=(nb,), in_specs=in_specs, out_specs=out_specs, out_shape=out_shape,
        compiler_params=_params(("parallel",)),
        name="readout_outproj",
    )(*args)


def _swiglu_tile(h, w1_ref, w3_ref, w2_ref):
    a = jnp.dot(h, w1_ref[...], preferred_element_type=F32)
    b = jnp.dot(h, w3_ref[...], preferred_element_type=F32)
    s = a * _sigmoid(a) * b
    return jnp.dot(s.astype(BF16), w2_ref[...], preferred_element_type=F32)


def _ffn_kernel(h_ref, x1_ref, w1_ref, w3_ref, w2_ref, mod_ref, g_ref, b_ref, o_ref, acc_ref, *, alpha, ctx_len,
                m_blocks_per_batch):
    f = pl.program_id(1)

    @pl.when(f == 0)
    def _():
        acc_ref[...] = jnp.zeros_like(acc_ref)

    acc_ref[...] += _swiglu_tile(h_ref[...], w1_ref, w3_ref, w2_ref)

    @pl.when(f == pl.num_programs(1) - 1)
    def _():
        tm = acc_ref.shape[0]
        rows = lax.broadcasted_iota(jnp.int32, acc_ref.shape, 0)
        row_in_batch = rows + (pl.program_id(0) % m_blocks_per_batch) * tm
        gate2 = jnp.where(row_in_batch < ctx_len, mod_ref[0, 5:6, :], mod_ref[1, 5:6, :])
        o_ref[...] = _layer_norm(alpha * x1_ref[...] + gate2 * acc_ref[...], g_ref[...], b_ref[...])


def _ffn(h2, x1, w1, w3, w2, modtab, ln_g, ln_b, *, alpha, ctx_len, batch):
    ntok, d = x1.shape
    dff = w1.shape[1]
    mpb = FFN_M_BLOCKS_PER_BATCH
    tm = ntok // batch // mpb
    tf = min(FFN_F_BLOCK, dff)
    row = lambda cols: pl.BlockSpec((tm, cols), lambda m, f: (m, 0))
    const = lambda *shape: pl.BlockSpec(shape, lambda m, f: (0,) * len(shape))
    in_specs = [row(d), row(d),
                pl.BlockSpec((d, tf), lambda m, f: (0, f)),
                pl.BlockSpec((d, tf), lambda m, f: (0, f)),
                pl.BlockSpec((tf, d), lambda m, f: (f, 0)),
                pl.BlockSpec((None, 2, 6, d), lambda m, f: (m // mpb, 0, 0, 0)),
                const(1, d), const(1, d)]
    return pl.pallas_call(
        functools.partial(_ffn_kernel, alpha=alpha, ctx_len=ctx_len, m_blocks_per_batch=mpb),
        grid=(ntok // tm, dff // tf),
        in_specs=in_specs, out_specs=row(d),
        out_shape=jax.ShapeDtypeStruct((ntok, d), F32),
        scratch_shapes=[pltpu.VMEM((tm, d), F32)],
        compiler_params=_params(("parallel", "arbitrary")),
        name="dense_swiglu",
    )(h2, x1, w1, w3, w2, modtab, ln_g, ln_b)


def _route_plan(route, n_exp):
    ntok = route.shape[0]
    tm = MOE_ROW_TILE
    n_tiles_max = -(-(TOP_K * ntok + n_exp * (tm - 1)) // tm)
    expert = jnp.concatenate([route[:, ROUTE_E1], route[:, ROUTE_E2]]).astype(jnp.int32)
    onehot = (expert[:, None] == jnp.arange(n_exp, dtype=jnp.int32)[None, :]).astype(jnp.int32)
    running = jnp.cumsum(onehot, axis=0)
    rank = jnp.sum(running * onehot, axis=1) - 1
    tiles = (running[-1] + tm - 1) // tm
    tile_end = jnp.cumsum(tiles)
    slot = (tile_end - tiles)[expert] * tm + rank
    n_tiles = tile_end[-1]
    tile_ids = jnp.minimum(jnp.arange(n_tiles_max, dtype=jnp.int32), n_tiles - 1)
    tile_expert = jnp.sum((tile_ids[:, None] >= tile_end[None, :]).astype(jnp.int32), axis=1)
    token = jnp.arange(TOP_K * ntok, dtype=jnp.int32) % ntok
    slot_token = jnp.zeros((n_tiles_max * tm,), jnp.int32).at[slot].set(token)
    return tile_expert, n_tiles.reshape(1), slot_token, slot


def _moe_experts_kernel(te_ref, nt_ref, tok_ref, h_hbm, w1_ref, w3_ref, w2_ref, o_ref, rows_ref, hb_ref, acc_ref,
                        sem):
    t = pl.program_id(0)
    f = pl.program_id(1)
    tm = rows_ref.shape[0]
    used = t < nt_ref[0]

    def row_copy(r, src_row):
        return pltpu.make_async_copy(h_hbm.at[pl.ds(src_row, 1)], rows_ref.at[pl.ds(r, 1)], sem)

    @pl.when(used & (f == 0))
    def _():
        def start(r, carry):
            row_copy(r, tok_ref[t * tm + r]).start()
            return carry

        def wait(r, carry):
            row_copy(r, 0).wait()
            return carry

        lax.fori_loop(0, tm, start, 0, unroll=DMA_LOOP_UNROLL)
        lax.fori_loop(0, tm, wait, 0, unroll=DMA_LOOP_UNROLL)
        hb_ref[...] = rows_ref[...].astype(BF16)
        acc_ref[...] = jnp.zeros_like(acc_ref)

    @pl.when(used)
    def _():
        acc_ref[...] += _swiglu_tile(hb_ref[...], w1_ref, w3_ref, w2_ref)

    @pl.when(f == pl.num_programs(1) - 1)
    def _():
        @pl.when(used)
        def _():
            o_ref[...] = acc_ref[...]

        @pl.when(jnp.logical_not(used))
        def _():
            o_ref[...] = jnp.zeros_like(o_ref)


def _moe_experts(h2, plan, w1, w3, w2):
    tile_expert, n_tiles, slot_token, _ = plan
    ntok, d = h2.shape
    dff = w1.shape[2]
    tm = MOE_ROW_TILE
    tf = min(FFN_F_BLOCK, dff)
    nf = dff // tf
    f_of = lambda t, f, nt: jnp.where(t < nt[0], f, nf - 1)
    grid_spec = pltpu.PrefetchScalarGridSpec(
        num_scalar_prefetch=3,
        grid=(tile_expert.shape[0], nf),
        in_specs=[pl.BlockSpec(memory_space=pl.ANY),
                  pl.BlockSpec((None, d, tf), lambda t, f, te, nt, tk: (te[t], 0, f_of(t, f, nt))),
                  pl.BlockSpec((None, d, tf), lambda t, f, te, nt, tk: (te[t], 0, f_of(t, f, nt))),
                  pl.BlockSpec((None, tf, d), lambda t, f, te, nt, tk: (te[t], f_of(t, f, nt), 0))],
        out_specs=pl.BlockSpec((tm, d), lambda t, f, te, nt, tk: (t, 0)),
        scratch_shapes=[pltpu.VMEM((tm, d), F32), pltpu.VMEM((tm, d), BF16), pltpu.VMEM((tm, d), F32),
                        pltpu.SemaphoreType.DMA(())])
    return pl.pallas_call(
        _moe_experts_kernel, grid_spec=grid_spec,
        out_shape=jax.ShapeDtypeStruct((slot_token.shape[0], d), F32),
        compiler_params=_params(("arbitrary", "arbitrary")),
        name="moe_experts",
    )(tile_expert, n_tiles, slot_token, h2, w1, w3, w2)


def _moe_combine_kernel(slot_ref, y_hbm, x1_ref, route_ref, mod_ref, g_ref, b_ref, o_ref, rows_ref, sem, *, alpha,
                        ntok):
    i = pl.program_id(0)
    tb = x1_ref.shape[0]

    def row_copy(k, r, src_row):
        return pltpu.make_async_copy(y_hbm.at[pl.ds(src_row, 1)], rows_ref.at[k, pl.ds(r, 1)], sem)

    def start(r, carry):
        for k in range(TOP_K):
            row_copy(k, r, slot_ref[k * ntok + i * tb + r]).start()
        return carry

    def wait(r, carry):
        for k in range(TOP_K):
            row_copy(k, r, 0).wait()
        return carry

    lax.fori_loop(0, tb, start, 0, unroll=DMA_LOOP_UNROLL)
    lax.fori_loop(0, tb, wait, 0, unroll=DMA_LOOP_UNROLL)
    mix = (route_ref[:, ROUTE_W1:ROUTE_W1 + 1] * rows_ref[0] + route_ref[:, ROUTE_W2:ROUTE_W2 + 1] * rows_ref[1])
    o_ref[...] = _layer_norm(alpha * x1_ref[...] + mod_ref[5:6, :] * mix, g_ref[...], b_ref[...])


def _moe_combine(y_slots, plan, x1, route, modtab, ln_g, ln_b, *, alpha, blocks_per_batch):
    ntok, d = x1.shape
    slot = plan[3]
    row = lambda cols: pl.BlockSpec((TOK_BLOCK, cols), lambda i, s: (i, 0))
    const = lambda *shape: pl.BlockSpec(shape, lambda i, s: (0,) * len(shape))
    grid_spec = pltpu.PrefetchScalarGridSpec(
        num_scalar_prefetch=1,
        grid=(ntok // TOK_BLOCK,),
        in_specs=[pl.BlockSpec(memory_space=pl.ANY), row(d), row(LANE),
                  pl.BlockSpec((None, None, 6, d),
                               lambda i, s: (i // blocks_per_batch, jnp.minimum(i % blocks_per_batch, 1), 0, 0)),
                  const(1, d), const(1, d)],
        out_specs=row(d),
        scratch_shapes=[pltpu.VMEM((TOP_K, TOK_BLOCK, d), F32), pltpu.SemaphoreType.DMA(())])
    return pl.pallas_call(
        functools.partial(_moe_combine_kernel, alpha=alpha, ntok=ntok), grid_spec=grid_spec,
        out_shape=jax.ShapeDtypeStruct((ntok, d), F32),
        compiler_params=_params(("arbitrary",)),
        name="moe_combine",
    )(slot, y_slots, x1, route, modtab, ln_g, ln_b)


def _head_ones(width):
    hid = jnp.arange(width) // HEAD
    return (hid[:, None] == hid[None, :]).astype(BF16)


def _stack_lora(w):
    _, rank, width = w.shape
    z = jnp.zeros((rank, width), w.dtype)
    return jnp.concatenate([jnp.concatenate([w[0], z], axis=1), jnp.concatenate([z, w[1]], axis=1)], axis=0)


def kernel(x, c, ctx, c_ctx, w_mod, b_mod, w_in, rk_w0, rk_w2, rk_a0, rk_a2, rk_kk, rk_ka, rk_rk, rk_g2,
           rk_lnx_g, rk_lnx_b, conv_w, w_out, ln1_g, ln1_b, ln2_g, ln2_b, ffn_w1, ffn_w3, ffn_w2,
           router_w, router_b, moe_w1, moe_w3, moe_w2):
    batch, seq, d = x.shape
    ctx_len = ctx.shape[1]
    depth = w_in.shape[0]
    width = rk_kk.shape[-1]
    n_exp = router_w.shape[-1]
    tok = ctx_len + seq
    assert ctx_len == TOK_BLOCK and seq % TOK_BLOCK == 0 and seq % GRID_W == 0 and batch < 8
    assert 2 * rk_w2.shape[2] == LANE and 2 * rk_a2.shape[2] == LANE and rk_g2.shape[1] == LANE
    assert width % (GROUP * HEAD) == 0 and n_exp <= LANE and tok % (FFN_M_BLOCKS_PER_BATCH * 16) == 0
    blocks_per_batch = tok // TOK_BLOCK
    alpha = (2 * depth) ** 0.25

    cc = jnp.concatenate([c, c_ctx[None, :], jnp.zeros((8 - batch - 1, d), F32)], axis=0)
    mods = _modulation(cc, w_mod, b_mod).reshape(depth, 8, 6, d)
    modtab = jnp.stack([jnp.broadcast_to(mods[:, batch][:, None], (depth, batch, 6, d)), mods[:, :batch]],
                       axis=2)

    xs = jnp.concatenate([ctx, x], axis=1).reshape(batch * tok, d)
    ones = _head_ones(width)
    for l in range(depth):
        shared, per_dir, extras = _inproj(
            xs, modtab[l], w_in[l].astype(BF16), _stack_lora(rk_w2[l]).astype(BF16),
            _stack_lora(rk_a2[l]).astype(BF16), rk_w0[l].reshape(1, 2 * width), rk_a0[l].reshape(1, 2 * width),
            rk_kk[l][None], rk_ka[l][None], rk_rk[l].reshape(1, width), rk_g2[l].astype(BF16), conv_w[l], ones,
            blocks_per_batch=blocks_per_batch)
        yf, yb = _wkv_scan(shared, per_dir, batch=batch, ctx_chunks=ctx_len // CHUNK)
        i = l // 2
        router = None
        if l % 2 == 1:
            router = (jnp.pad(router_w[i], ((0, 0), (0, LANE - n_exp))),
                      jnp.pad(router_b[i], (0, LANE - n_exp), constant_values=PAD_LOGIT)[None])
        res = _outproj(yf, yb, extras, xs, modtab[l], rk_lnx_g[l][None], rk_lnx_b[l][None],
                       w_out[l].astype(BF16), ln1_g[l][None], ln1_b[l][None], ones, router,
                       alpha=alpha, blocks_per_batch=blocks_per_batch)
        if l % 2 == 0:
            x1, h2 = res
            xs = _ffn(h2, x1, ffn_w1[i].astype(BF16), ffn_w3[i].astype(BF16), ffn_w2[i].astype(BF16), modtab[l],
                      ln2_g[l][None], ln2_b[l][None], alpha=alpha, ctx_len=ctx_len, batch=batch)
        else:
            x1, h2, route = res
            plan = _route_plan(route, n_exp)
            y_slots = _moe_experts(h2, plan, moe_w1[i].astype(BF16), moe_w3[i].astype(BF16),
                                   moe_w2[i].astype(BF16))
            xs = _moe_combine(y_slots, plan, x1, route, modtab[l], ln2_g[l][None], ln2_b[l][None],
                              alpha=alpha, blocks_per_batch=blocks_per_batch)
    return xs.reshape(batch, tok, d)[:, ctx_len:]
```

```python
import functools

import jax
import jax.numpy as jnp
from jax import lax
from jax.experimental import pallas as pl
from jax.experimental.pallas import tpu as pltpu

F32 = jnp.float32
BF16 = jnp.bfloat16
HIGHEST = lax.Precision.HIGHEST

HEAD = 64
GRID_W = 64
CONV_K = 3
TOP_K = 2
LN_EPS = 1e-5
GN_EPS = 64e-5
NORM_EPS = 1e-12
PAD_LOGIT = -1e30
ROUTE_E1, ROUTE_E2, ROUTE_W1, ROUTE_W2 = 0, 1, 2, 3
MOE_ROW_TILE = 512
DMA_LOOP_UNROLL = 8
MXU_WIDTH_V7X = 256
GROUP = MXU_WIDTH_V7X // HEAD
CHUNK = MXU_WIDTH_V7X // GROUP
SCAN_BATCH_BLOCK = 4
TOK_BLOCK = 256
LANE = 128
VMEM_LIMIT_V7X = 56 * 1024 * 1024
FFN_M_BLOCKS_PER_BATCH = 4
FFN_F_BLOCK = 512
MOD_N_BLOCK = 1536


def _mm(a, b):
    return jnp.dot(a.astype(BF16), b.astype(BF16), preferred_element_type=F32)


def _mm_nt(a, b):
    return lax.dot_general(a.astype(BF16), b.astype(BF16), (((1,), (1,)), ((), ())),
                           preferred_element_type=F32)


def _sigmoid(x):
    return 1.0 / (1.0 + jnp.exp(-x))


def _layer_norm(z, g, b):
    mu = jnp.mean(z, axis=-1, keepdims=True)
    d = z - mu
    var = jnp.mean(d * d, axis=-1, keepdims=True)
    return d * lax.rsqrt(var + LN_EPS) * g + b


def _params(sem):
    return pltpu.CompilerParams(dimension_semantics=sem, vmem_limit_bytes=VMEM_LIMIT_V7X)


def _mod_kernel(c_ref, w_ref, b_ref, o_ref):
    c = c_ref[...]
    s = c * _sigmoid(c)
    o_ref[...] = jnp.dot(s, w_ref[...], precision=HIGHEST, preferred_element_type=F32) + b_ref[...]


def _modulation(cc, w_mod, b_mod):
    depth, d, n = w_mod.shape
    tn = MOD_N_BLOCK
    return pl.pallas_call(
        _mod_kernel,
        grid=(depth, n // tn),
        in_specs=[pl.BlockSpec((8, d), lambda l, j: (0, 0)),
                  pl.BlockSpec((None, d, tn), lambda l, j: (l, 0, j)),
                  pl.BlockSpec((None, 1, tn), lambda l, j: (l, 0, j))],
        out_specs=pl.BlockSpec((None, 8, tn), lambda l, j: (l, 0, j)),
        out_shape=jax.ShapeDtypeStruct((depth, 8, n), F32),
        compiler_params=_params(("parallel", "parallel")),
        name="adaln_modulation",
    )(cc, w_mod, b_mod.reshape(depth, 1, n))


def _inproj_kernel(x_ref, mod_ref, w_ref, w2s_ref, a2s_ref, w0_ref, a0_ref, kk_ref, ka_ref, rk_ref, g2_ref,
                   cw_ref, ones_ref, sh_ref, dir_ref, ex_ref, *, width, blocks_per_batch):
    wd_ = width
    x = x_ref[...]
    h = x * (1.0 + mod_ref[1:2, :]) + mod_ref[0:1, :]
    p = _mm(h, w_ref[...])
    r = p[:, 0:wd_]
    k = p[:, wd_:2 * wd_]
    v = p[:, 2 * wd_:3 * wd_]
    o = 3 * wd_
    wd = p[:, o:o + LANE]
    ad = p[:, o + LANE:o + 2 * LANE]
    gd = p[:, o + 2 * LANE:o + 3 * LANE]
    o = o + 3 * LANE
    cb = p[:, o:o + wd_]
    cc = p[:, o + wd_:o + 2 * wd_]
    ch = p[:, o + 2 * wd_:o + 3 * wd_]
    ones = ones_ref[...]

    z = w0_ref[...] + _mm(jnp.tanh(wd), w2s_ref[...])
    softplus_neg = jnp.maximum(-z, 0.0) + jnp.log(1.0 + jnp.exp(-jnp.abs(z)))
    w_log = -softplus_neg - 0.5
    log_decay = -jnp.exp(w_log)
    a_all = _sigmoid(a0_ref[...] + _mm(ad, a2s_ref[...]))

    kk_raw = k * kk_ref[...]
    sq = kk_raw * kk_raw
    sq_hi = sq.astype(BF16)
    sq_lo = (sq - sq_hi.astype(F32)).astype(BF16)
    ss = (jnp.dot(sq_hi, ones, preferred_element_type=F32) + jnp.dot(sq_lo, ones, preferred_element_type=F32))
    kk = kk_raw / jnp.maximum(jnp.sqrt(ss), NORM_EPS)

    sh_ref[:, 0:wd_] = r
    sh_ref[:, wd_:2 * wd_] = kk
    sh_ref[:, 2 * wd_:3 * wd_] = v
    kd_sum = jnp.zeros_like(k)
    for d in range(2):
        a = a_all[:, d * wd_:(d + 1) * wd_]
        kd = k * (1.0 + (a - 1.0) * ka_ref[...])
        kd_sum = kd_sum + kd
        dir_ref[d, :, 0:wd_] = log_decay[:, d * wd_:(d + 1) * wd_]
        dir_ref[d, :, wd_:2 * wd_] = kd
        dir_ref[d, :, 2 * wd_:3 * wd_] = kk * a
    bonus = _mm(r * kd_sum * rk_ref[...], ones) * v
    gate = _mm(_sigmoid(gd), g2_ref[...])

    u = cc * ch
    rows = lax.broadcasted_iota(jnp.int32, u.shape, 0)
    is_ctx = (pl.program_id(0) % blocks_per_batch) == 0
    period_mask = jnp.where(is_ctx, TOK_BLOCK - 1, GRID_W - 1)
    pos = rows & period_mask
    u_prev = jnp.where(pos == 0, 0.0, pltpu.roll(u, 1, 0))
    u_next = jnp.where(pos == period_mask, 0.0, pltpu.roll(u, TOK_BLOCK - 1, 0))
    conv = cb * (u_prev * cw_ref[0:1, :] + u * cw_ref[1:2, :] + u_next * cw_ref[2:3, :])

    ex_ref[:, 0:wd_] = bonus
    ex_ref[:, wd_:2 * wd_] = gate
    ex_ref[:, 2 * wd_:3 * wd_] = conv


def _inproj(x, modtab, w_in, w2s, a2s, w0, a0, k_k, k_a, r_k, g2, conv_w, ones, *, blocks_per_batch):
    ntok, d = x.shape
    width = k_k.shape[-1]
    pcols = w_in.shape[1]
    nb = ntok // TOK_BLOCK
    const = lambda *shape: pl.BlockSpec(shape, lambda i: (0,) * len(shape))
    return pl.pallas_call(
        functools.partial(_inproj_kernel, width=width, blocks_per_batch=blocks_per_batch),
        grid=(nb,),
        in_specs=[pl.BlockSpec((TOK_BLOCK, d), lambda i: (i, 0)),
                  pl.BlockSpec((None, None, 6, d),
                               lambda i: (i // blocks_per_batch, jnp.minimum(i % blocks_per_batch, 1), 0, 0)),
                  const(d, pcols), const(LANE, 2 * width), const(LANE, 2 * width), const(1, 2 * width),
                  const(1, 2 * width), const(1, width), const(1, width), const(1, width), const(LANE, width),
                  const(CONV_K, width), const(width, width)],
        out_specs=[pl.BlockSpec((TOK_BLOCK, 3 * width), lambda i: (i, 0)),
                   pl.BlockSpec((2, TOK_BLOCK, 3 * width), lambda i: (0, i, 0)),
                   pl.BlockSpec((TOK_BLOCK, 3 * width), lambda i: (i, 0))],
        out_shape=[jax.ShapeDtypeStruct((ntok, 3 * width), F32),
                   jax.ShapeDtypeStruct((2, ntok, 3 * width), F32),
                   jax.ShapeDtypeStruct((ntok, 3 * width), F32)],
        compiler_params=_params(("parallel",)),
        name="inproj_prep",
    )(x, modtab, w_in, w2s, a2s, w0, a0, k_k, k_a, r_k, g2, conv_w, ones)


def _scan_masks(rev):
    n = GROUP * CHUNK
    ri = lax.broadcasted_iota(jnp.int32, (n, n), 0)
    ci = lax.broadcasted_iota(jnp.int32, (n, n), 1)
    same_head = (ri // HEAD) == (ci // HEAD)
    t = lax.broadcasted_iota(jnp.int32, (CHUNK, n), 0)
    s = lax.broadcasted_iota(jnp.int32, (CHUNK, n), 1) % CHUNK
    before = (s > t) if rev else (s < t)
    levels = []
    size = 1
    while size < CHUNK:
        levels.append(((t // (2 * size)) == (s // (2 * size))) & ((t // size) != (s // size)))
        size *= 2
    return same_head, before, before | (s == t), s == t, levels


def _scan_chunks(streams):
    n = GROUP * CHUNK
    ti = lax.broadcasted_iota(jnp.int32, (CHUNK, CHUNK), 0)
    si = lax.broadcasted_iota(jnp.int32, (CHUNK, CHUNK), 1)

    def blk(m, same_head):
        return jnp.where(same_head, jnp.concatenate([m] * GROUP, axis=0), 0.0).astype(BF16)

    def prepare(r, kk, v, lw, kd, b, t_prev, masks, rev):
        same_head = masks[0]
        tri = jnp.where((si >= ti) if rev else (si <= ti), 1.0, 0.0).astype(BF16)
        lw_hi = lw.astype(BF16)
        lw_rest = lw - lw_hi.astype(F32)
        lw_mid = lw_rest.astype(BF16)
        lw_lo = (lw_rest - lw_mid.astype(F32)).astype(BF16)
        g_parts = jnp.dot(tri, jnp.concatenate([lw_hi, lw_mid, lw_lo], axis=1), preferred_element_type=F32)
        g_inc = g_parts[:, 0:n] + g_parts[:, n:2 * n] + g_parts[:, 2 * n:3 * n]
        g_exc = g_inc - lw
        g_end = g_inc[0:1, :] if rev else g_inc[CHUNK - 1:CHUNK, :]
        e_neg = jnp.exp(-g_inc)
        e_end = jnp.exp(g_end - g_inc)
        decay = jnp.exp(jnp.broadcast_to(g_end, (LANE, n)).T)
        return dict(kk=(kk * jnp.exp(g_exc)).astype(BF16), r=(r * jnp.exp(g_inc)).astype(BF16),
                    bk=jnp.concatenate([blk(b * e_neg, same_head), blk(kd * e_neg, same_head)], axis=0),
                    ends_t=jnp.concatenate([b * e_end, kd * e_end], axis=0).T.astype(BF16),
                    v=v.astype(BF16), v_blk=blk(v, same_head),
                    decay=jnp.concatenate([decay] * (n // LANE), axis=1), t=t_prev, t_b=t_prev.astype(BF16))

    ops = [prepare(*st) for st in streams]
    masks = [st[7] for st in streams]
    pairs = [_mm_nt(jnp.concatenate([o["kk"], o["r"]], axis=0), o["bk"]) for o in ops]
    l_b = [jnp.where(m[1], p[0:CHUNK, 0:n], 0.0) for p, m in zip(pairs, masks)]
    l_k = [jnp.where(m[1], p[0:CHUNK, n:2 * n], 0.0).astype(BF16) for p, m in zip(pairs, masks)]
    a_r = [jnp.where(jnp.concatenate([m[2], m[2]], axis=1), p[CHUNK:2 * CHUNK, :], 0.0).astype(BF16)
           for p, m in zip(pairs, masks)]

    inv = [jnp.where(m[3], 1.0, 0.0) - jnp.where(m[4][0], lb, 0.0) for lb, m in zip(l_b, masks)]
    for lvl in range(1, len(masks[0][4])):
        half = [_mm(jnp.where(m[4][lvl], lb, 0.0), blk(a, m[0])) for lb, a, m in zip(l_b, inv, masks)]
        inv = [a - _mm(a, blk(h, m[0])) for a, h, m in zip(inv, half, masks)]

    x = [-_mm(jnp.concatenate([o["kk"], lk], axis=1), jnp.concatenate([o["t_b"], o["v_blk"]], axis=0))
         for o, lk in zip(ops, l_k)]
    u = [_mm(a, blk(xx, m[0])) for a, xx, m in zip(inv, x, masks)]
    ys = [_mm(jnp.concatenate([o["r"], ar], axis=1),
              jnp.concatenate([o["t_b"], blk(uu, m[0]), o["v_blk"]], axis=0))
          for o, ar, uu, m in zip(ops, a_r, u, masks)]
    t_new = [o["decay"] * o["t"]
             + jnp.where(m[0], _mm(o["ends_t"], jnp.concatenate([uu.astype(BF16), o["v"]], axis=0)), 0.0)
             for o, uu, m in zip(ops, u, masks)]
    return ys, t_new


def _scan_kernel(shf_ref, shb_ref, df_ref, db_ref, yf_ref, yb_ref, s_ref, *, width):
    @pl.when(pl.program_id(1) == 0)
    def _():
        s_ref[...] = jnp.zeros_like(s_ref)

    gw = GROUP * HEAD
    n_groups = width // gw
    streams, sinks = [], []
    for d, (sh, dr, yo) in enumerate(((shf_ref, df_ref, yf_ref), (shb_ref, db_ref, yb_ref))):
        rev = d == 1
        masks = _scan_masks(rev)
        for j in range(sh.shape[0]):
            for g in range(n_groups):
                lo = g * gw
                idx = (d * sh.shape[0] + j) * n_groups + g
                streams.append((sh[j, :, lo:lo + gw], sh[j, :, width + lo:width + lo + gw],
                                sh[j, :, 2 * width + lo:2 * width + lo + gw],
                                dr[j, :, lo:lo + gw], dr[j, :, width + lo:width + lo + gw],
                                dr[j, :, 2 * width + lo:2 * width + lo + gw],
                                s_ref[idx], masks, rev))
                sinks.append((yo, j, lo, idx))
    ys, s_new = _scan_chunks(streams)
    for (yo, j, lo, idx), y, s in zip(sinks, ys, s_new):
        yo[j, :, lo:lo + gw] = y
        s_ref[idx] = s


def _wkv_scan(shared, per_dir, *, batch, ctx_chunks):
    ntok, w3 = shared.shape
    width = w3 // 3
    tok = ntok // batch
    nc = tok // CHUNK
    n_groups = width // (GROUP * HEAD)
    bb = SCAN_BATCH_BLOCK if batch % SCAN_BATCH_BLOCK == 0 else 1
    shared = shared.reshape(batch, tok, w3)
    per_dir = per_dir.reshape(2, batch, tok, w3)

    def bwd(i):
        return jnp.where(i < ctx_chunks, ctx_chunks - 1 - i, nc - 1 - (i - ctx_chunks))

    yf, yb = pl.pallas_call(
        functools.partial(_scan_kernel, width=width),
        grid=(batch // bb, nc),
        in_specs=[pl.BlockSpec((bb, CHUNK, w3), lambda b, i: (b, i, 0)),
                  pl.BlockSpec((bb, CHUNK, w3), lambda b, i: (b, bwd(i), 0)),
                  pl.BlockSpec((None, bb, CHUNK, w3), lambda b, i: (0, b, i, 0)),
                  pl.BlockSpec((None, bb, CHUNK, w3), lambda b, i: (1, b, bwd(i), 0))],
        out_specs=[pl.BlockSpec((bb, CHUNK, width), lambda b, i: (b, i, 0)),
                   pl.BlockSpec((bb, CHUNK, width), lambda b, i: (b, bwd(i), 0))],
        out_shape=[jax.ShapeDtypeStruct((batch, tok, width), F32)] * 2,
        scratch_shapes=[pltpu.VMEM((2 * bb * n_groups, GROUP * HEAD, GROUP * HEAD), F32)],
        compiler_params=_params(("parallel", "arbitrary")),
        name="wkv_scan",
    )(shared, shared, per_dir, per_dir)
    return yf.reshape(ntok, width), yb.reshape(ntok, width)


def _top2_route(logits):
    lane = lax.broadcasted_iota(jnp.int32, logits.shape, 1)
    n = logits.shape[-1]
    m1 = jnp.max(logits, axis=-1, keepdims=True)
    i1 = jnp.min(jnp.where(logits == m1, lane, n), axis=-1, keepdims=True)
    rest = jnp.where(lane == i1, PAD_LOGIT, logits)
    m2 = jnp.max(rest, axis=-1, keepdims=True)
    i2 = jnp.min(jnp.where(rest == m2, lane, n), axis=-1, keepdims=True)
    e = jnp.exp(m2 - m1)
    rec = jnp.where(lane == ROUTE_W2, e / (1.0 + e), 0.0)
    rec = jnp.where(lane == ROUTE_W1, 1.0 / (1.0 + e), rec)
    rec = jnp.where(lane == ROUTE_E2, i2.astype(F32), rec)
    return jnp.where(lane == ROUTE_E1, i1.astype(F32), rec)


def _outproj_kernel(*refs, width, alpha, moe):
    if moe:
        (yf_ref, yb_ref, ex_ref, x_ref, mod_ref, lg_ref, lb_ref, wo_ref, g1_ref, b1_ref, ones_ref,
         rw_ref, rb_ref, x1_ref, h2_ref, gate_ref) = refs
    else:
        (yf_ref, yb_ref, ex_ref, x_ref, mod_ref, lg_ref, lb_ref, wo_ref, g1_ref, b1_ref, ones_ref,
         x1_ref, h2_ref) = refs
    ones = ones_ref[...]
    y = yf_ref[...] + yb_ref[...]
    inv_n = 1.0 / HEAD
    mu = _mm(y, ones) * inv_n
    dlt = y - mu
    var = _mm(dlt * dlt, ones) * inv_n
    yn = dlt * lax.rsqrt(var + GN_EPS)
    bonus = ex_ref[:, 0:width]
    gate = ex_ref[:, width:2 * width]
    conv = ex_ref[:, 2 * width:3 * width]
    rwkv = (yn * lg_ref[...] + lb_ref[...] + bonus) * gate
    cat = jnp.concatenate([rwkv.astype(BF16), conv.astype(BF16)], axis=1)
    mix = jnp.dot(cat, wo_ref[...], preferred_element_type=F32)
    x1 = _layer_norm(alpha * x_ref[...] + mod_ref[2:3, :] * mix, g1_ref[...], b1_ref[...])
    h2 = x1 * (1.0 + mod_ref[4:5, :]) + mod_ref[3:4, :]
    x1_ref[...] = x1
    h2_ref[...] = h2.astype(h2_ref.dtype)
    if moe:
        logits = jnp.dot(h2, rw_ref[...], precision=HIGHEST, preferred_element_type=F32) + rb_ref[...]
        gate_ref[...] = _top2_route(logits)


def _outproj(yf, yb, extras, x, modtab, lnx_g, lnx_b, w_out, ln_g, ln_b, ones, router, *, alpha,
             blocks_per_batch):
    ntok, d = x.shape
    width = yf.shape[1]
    nb = ntok // TOK_BLOCK
    moe = router is not None
    const = lambda *shape: pl.BlockSpec(shape, lambda i: (0,) * len(shape))
    row = lambda cols: pl.BlockSpec((TOK_BLOCK, cols), lambda i: (i, 0))
    in_specs = [row(width), row(width), row(3 * width), row(d),
                pl.BlockSpec((None, None, 6, d),
                             lambda i: (i // blocks_per_batch, jnp.minimum(i % blocks_per_batch, 1), 0, 0)),
                const(1, width), const(1, width), const(2 * width, d), const(1, d), const(1, d),
                const(width, width)]
    out_specs = [row(d), row(d)]
    out_shape = [jax.ShapeDtypeStruct((ntok, d), F32), jax.ShapeDtypeStruct((ntok, d), F32 if moe else BF16)]
    args = [yf, yb, extras, x, modtab, lnx_g, lnx_b, w_out, ln_g, ln_b, ones]
    if moe:
        in_specs += [const(d, LANE), const(1, LANE)]
        out_specs.append(row(LANE))
        out_shape.append(jax.ShapeDtypeStruct((ntok, LANE), F32))
        args += list(router)
    return pl.pallas_call(
        functools.partial(_outproj_kernel, width=width, alpha=alpha, moe=moe),
        grid=(nb,), in_specs=in_specs, out_specs=out_specs, out_shape=out_shape,
        compiler_params=_params(("parallel",)),
        name="readout_outproj",
    )(*args)


def _swiglu_tile(h, w1_ref, w3_ref, w2_ref):
    a = jnp.dot(h, w1_ref[...], preferred_element_type=F32)
    b = jnp.dot(h, w3_ref[...], preferred_element_type=F32)
    s = a * _sigmoid(a) * b
    return jnp.dot(s.astype(BF16), w2_ref[...], preferred_element_type=F32)


def _ffn_kernel(h_ref, x1_ref, w1_ref, w3_ref, w2_ref, mod_ref, g_ref, b_ref, o_ref, acc_ref, *, alpha, ctx_len,
                m_blocks_per_batch):
    f = pl.program_id(1)

    @pl.when(f == 0)
    def _():
        acc_ref[...] = jnp.zeros_like(acc_ref)

    acc_ref[...] += _swiglu_tile(h_ref[...], w1_ref, w3_ref, w2_ref)

    @pl.when(f == pl.num_programs(1) - 1)
    def _():
        tm = acc_ref.shape[0]
        rows = lax.broadcasted_iota(jnp.int32, acc_ref.shape, 0)
        row_in_batch = rows + (pl.program_id(0) % m_blocks_per_batch) * tm
        gate2 = jnp.where(row_in_batch < ctx_len, mod_ref[0, 5:6, :], mod_ref[1, 5:6, :])
        o_ref[...] = _layer_norm(alpha * x1_ref[...] + gate2 * acc_ref[...], g_ref[...], b_ref[...])


def _ffn(h2, x1, w1, w3, w2, layer, modtab, ln_g, ln_b, *, alpha, ctx_len, batch):
    ntok, d = x1.shape
    dff = w1.shape[2]
    mpb = FFN_M_BLOCKS_PER_BATCH
    tm = ntok // batch // mpb
    tf = min(FFN_F_BLOCK, dff)
    row = lambda cols: pl.BlockSpec((tm, cols), lambda m, f: (m, 0))
    const = lambda *shape: pl.BlockSpec(shape, lambda m, f: (0,) * len(shape))
    in_specs = [row(d), row(d),
                pl.BlockSpec((None, d, tf), lambda m, f: (layer, 0, f)),
                pl.BlockSpec((None, d, tf), lambda m, f: (layer, 0, f)),
                pl.BlockSpec((None, tf, d), lambda m, f: (layer, f, 0)),
                pl.BlockSpec((None, 2, 6, d), lambda m, f: (m // mpb, 0, 0, 0)),
                const(1, d), const(1, d)]
    return pl.pallas_call(
        functools.partial(_ffn_kernel, alpha=alpha, ctx_len=ctx_len, m_blocks_per_batch=mpb),
        grid=(ntok // tm, dff // tf),
        in_specs=in_specs, out_specs=row(d),
        out_shape=jax.ShapeDtypeStruct((ntok, d), F32),
        scratch_shapes=[pltpu.VMEM((tm, d), F32)],
        compiler_params=_params(("parallel", "arbitrary")),
        name="dense_swiglu",
    )(h2, x1, w1, w3, w2, modtab, ln_g, ln_b)


def _route_plan(route, n_exp):
    ntok = route.shape[0]
    tm = MOE_ROW_TILE
    n_tiles_max = -(-(TOP_K * ntok + n_exp * (tm - 1)) // tm)
    expert = jnp.concatenate([route[:, ROUTE_E1], route[:, ROUTE_E2]]).astype(jnp.int32)
    onehot = (expert[:, None] == jnp.arange(n_exp, dtype=jnp.int32)[None, :]).astype(jnp.int32)
    running = jnp.cumsum(onehot, axis=0)
    rank = jnp.sum(running * onehot, axis=1) - 1
    tiles = (running[-1] + tm - 1) // tm
    tile_end = jnp.cumsum(tiles)
    slot = (tile_end - tiles)[expert] * tm + rank
    n_tiles = tile_end[-1]
    tile_ids = jnp.minimum(jnp.arange(n_tiles_max, dtype=jnp.int32), n_tiles - 1)
    tile_expert = jnp.sum((tile_ids[:, None] >= tile_end[None, :]).astype(jnp.int32), axis=1)
    token = jnp.arange(TOP_K * ntok, dtype=jnp.int32) % ntok
    slot_token = jnp.zeros((n_tiles_max * tm,), jnp.int32).at[slot].set(token)
    return tile_expert, n_tiles.reshape(1), slot_token, slot


def _moe_experts_kernel(te_ref, nt_ref, tok_ref, h_hbm, w1_ref, w3_ref, w2_ref, o_ref, rows_ref, hb_ref, acc_ref,
                        sem):
    t = pl.program_id(0)
    f = pl.program_id(1)
    tm = rows_ref.shape[0]
    used = t < nt_ref[0]

    def row_copy(r, src_row):
        return pltpu.make_async_copy(h_hbm.at[pl.ds(src_row, 1)], rows_ref.at[pl.ds(r, 1)], sem)

    @pl.when(used & (f == 0))
    def _():
        def start(r, carry):
            row_copy(r, tok_ref[t * tm + r]).start()
            return carry

        def wait(r, carry):
            row_copy(r, 0).wait()
            return carry

        lax.fori_loop(0, tm, start, 0, unroll=DMA_LOOP_UNROLL)
        lax.fori_loop(0, tm, wait, 0, unroll=DMA_LOOP_UNROLL)
        hb_ref[...] = rows_ref[...].astype(BF16)
        acc_ref[...] = jnp.zeros_like(acc_ref)

    @pl.when(used)
    def _():
        acc_ref[...] += _swiglu_tile(hb_ref[...], w1_ref, w3_ref, w2_ref)

    @pl.when(f == pl.num_programs(1) - 1)
    def _():
        @pl.when(used)
        def _():
            o_ref[...] = acc_ref[...]

        @pl.when(jnp.logical_not(used))
        def _():
            o_ref[...] = jnp.zeros_like(o_ref)


def _moe_experts(h2, plan, w1, w3, w2, layer):
    tile_expert, n_tiles, slot_token, _ = plan
    ntok, d = h2.shape
    dff = w1.shape[3]
    tm = MOE_ROW_TILE
    tf = min(FFN_F_BLOCK, dff)
    nf = dff // tf
    f_of = lambda t, f, nt: jnp.where(t < nt[0], f, nf - 1)
    grid_spec = pltpu.PrefetchScalarGridSpec(
        num_scalar_prefetch=3,
        grid=(tile_expert.shape[0], nf),
        in_specs=[pl.BlockSpec(memory_space=pl.ANY),
                  pl.BlockSpec((None, None, d, tf), lambda t, f, te, nt, tk: (layer, te[t], 0, f_of(t, f, nt))),
                  pl.BlockSpec((None, None, d, tf), lambda t, f, te, nt, tk: (layer, te[t], 0, f_of(t, f, nt))),
                  pl.BlockSpec((None, None, tf, d), lambda t, f, te, nt, tk: (layer, te[t], f_of(t, f, nt), 0))],
        out_specs=pl.BlockSpec((tm, d), lambda t, f, te, nt, tk: (t, 0)),
        scratch_shapes=[pltpu.VMEM((tm, d), F32), pltpu.VMEM((tm, d), BF16), pltpu.VMEM((tm, d), F32),
                        pltpu.SemaphoreType.DMA(())])
    return pl.pallas_call(
        _moe_experts_kernel, grid_spec=grid_spec,
        out_shape=jax.ShapeDtypeStruct((slot_token.shape[0], d), F32),
        compiler_params=_params(("arbitrary", "arbitrary")),
        name="moe_experts",
    )(tile_expert, n_tiles, slot_token, h2, w1, w3, w2)


def _moe_combine_kernel(slot_ref, y_hbm, x1_ref, route_ref, mod_ref, g_ref, b_ref, o_ref, rows_ref, sem, *, alpha,
                        ntok):
    i = pl.program_id(0)
    tb = x1_ref.shape[0]

    def row_copy(k, r, src_row):
        return pltpu.make_async_copy(y_hbm.at[pl.ds(src_row, 1)], rows_ref.at[k, pl.ds(r, 1)], sem)

    def start(r, carry):
        for k in range(TOP_K):
            row_copy(k, r, slot_ref[k * ntok + i * tb + r]).start()
        return carry

    def wait(r, carry):
        for k in range(TOP_K):
            row_copy(k, r, 0).wait()
        return carry

    lax.fori_loop(0, tb, start, 0, unroll=DMA_LOOP_UNROLL)
    lax.fori_loop(0, tb, wait, 0, unroll=DMA_LOOP_UNROLL)
    mix = (route_ref[:, ROUTE_W1:ROUTE_W1 + 1] * rows_ref[0] + route_ref[:, ROUTE_W2:ROUTE_W2 + 1] * rows_ref[1])
    o_ref[...] = _layer_norm(alpha * x1_ref[...] + mod_ref[5:6, :] * mix, g_ref[...], b_ref[...])


def _moe_combine(y_slots, plan, x1, route, modtab, ln_g, ln_b, *, alpha, blocks_per_batch):
    ntok, d = x1.shape
    slot = plan[3]
    row = lambda cols: pl.BlockSpec((TOK_BLOCK, cols), lambda i, s: (i, 0))
    const = lambda *shape: pl.BlockSpec(shape, lambda i, s: (0,) * len(shape))
    grid_spec = pltpu.PrefetchScalarGridSpec(
        num_scalar_prefetch=1,
        grid=(ntok // TOK_BLOCK,),
        in_specs=[pl.BlockSpec(memory_space=pl.ANY), row(d), row(LANE),
                  pl.BlockSpec((None, None, 6, d),
                               lambda i, s: (i // blocks_per_batch, jnp.minimum(i % blocks_per_batch, 1), 0, 0)),
                  const(1, d), const(1, d)],
        out_specs=row(d),
        scratch_shapes=[pltpu.VMEM((TOP_K, TOK_BLOCK, d), F32), pltpu.SemaphoreType.DMA(())])
    return pl.pallas_call(
        functools.partial(_moe_combine_kernel, alpha=alpha, ntok=ntok), grid_spec=grid_spec,
        out_shape=jax.ShapeDtypeStruct((ntok, d), F32),
        compiler_params=_params(("arbitrary",)),
        name="moe_combine",
    )(slot, y_slots, x1, route, modtab, ln_g, ln_b)


def _head_ones(width):
    hid = jnp.arange(width) // HEAD
    return (hid[:, None] == hid[None, :]).astype(BF16)


def _stack_lora(w):
    _, rank, width = w.shape
    z = jnp.zeros((rank, width), w.dtype)
    return jnp.concatenate([jnp.concatenate([w[0], z], axis=1), jnp.concatenate([z, w[1]], axis=1)], axis=0)


def kernel(x, c, ctx, c_ctx, w_mod, b_mod, w_in, rk_w0, rk_w2, rk_a0, rk_a2, rk_kk, rk_ka, rk_rk, rk_g2,
           rk_lnx_g, rk_lnx_b, conv_w, w_out, ln1_g, ln1_b, ln2_g, ln2_b, ffn_w1, ffn_w3, ffn_w2,
           router_w, router_b, moe_w1, moe_w3, moe_w2):
    batch, seq, d = x.shape
    ctx_len = ctx.shape[1]
    depth = w_in.shape[0]
    width = rk_kk.shape[-1]
    n_exp = router_w.shape[-1]
    tok = ctx_len + seq
    assert ctx_len == TOK_BLOCK and seq % TOK_BLOCK == 0 and seq % GRID_W == 0 and batch < 8
    assert 2 * rk_w2.shape[2] == LANE and 2 * rk_a2.shape[2] == LANE and rk_g2.shape[1] == LANE
    assert width % (GROUP * HEAD) == 0 and n_exp <= LANE and tok % (FFN_M_BLOCKS_PER_BATCH * 16) == 0
    blocks_per_batch = tok // TOK_BLOCK
    alpha = (2 * depth) ** 0.25

    cc = jnp.concatenate([c, c_ctx[None, :], jnp.zeros((8 - batch - 1, d), F32)], axis=0)
    mods = _modulation(cc, w_mod, b_mod).reshape(depth, 8, 6, d)
    modtab = jnp.stack([jnp.broadcast_to(mods[:, batch][:, None], (depth, batch, 6, d)), mods[:, :batch]],
                       axis=2)

    xs = jnp.concatenate([ctx, x], axis=1).reshape(batch * tok, d)
    ones = _head_ones(width)
    ffn_w = [w.astype(BF16) for w in (ffn_w1, ffn_w3, ffn_w2)]
    moe_w = [w.astype(BF16) for w in (moe_w1, moe_w3, moe_w2)]
    for l in range(depth):
        shared, per_dir, extras = _inproj(
            xs, modtab[l], w_in[l].astype(BF16), _stack_lora(rk_w2[l]).astype(BF16),
            _stack_lora(rk_a2[l]).astype(BF16), rk_w0[l].reshape(1, 2 * width), rk_a0[l].reshape(1, 2 * width),
            rk_kk[l][None], rk_ka[l][None], rk_rk[l].reshape(1, width), rk_g2[l].astype(BF16), conv_w[l], ones,
            blocks_per_batch=blocks_per_batch)
        yf, yb = _wkv_scan(shared, per_dir, batch=batch, ctx_chunks=ctx_len // CHUNK)
        i = l // 2
        router = None
        if l % 2 == 1:
            router = (jnp.pad(router_w[i], ((0, 0), (0, LANE - n_exp))),
                      jnp.pad(router_b[i], (0, LANE - n_exp), constant_values=PAD_LOGIT)[None])
        res = _outproj(yf, yb, extras, xs, modtab[l], rk_lnx_g[l][None], rk_lnx_b[l][None],
                       w_out[l].astype(BF16), ln1_g[l][None], ln1_b[l][None], ones, router,
                       alpha=alpha, blocks_per_batch=blocks_per_batch)
        if l % 2 == 0:
            x1, h2 = res
            xs = _ffn(h2, x1, *ffn_w, i, modtab[l], ln2_g[l][None], ln2_b[l][None], alpha=alpha, ctx_len=ctx_len,
                      batch=batch)
        else:
            x1, h2, route = res
            plan = _route_plan(route, n_exp)
            y_slots = _moe_experts(h2, plan, *moe_w, i)
            xs = _moe_combine(y_slots, plan, x1, route, modtab[l], ln2_g[l][None], ln2_b[l][None],
                              alpha=alpha, blocks_per_batch=blocks_per_batch)
    return xs.reshape(batch, tok, d)[:, ctx_len:]
```

```python
import functools

import jax
import jax.numpy as jnp
from jax import lax
from jax.experimental import pallas as pl
from jax.experimental.pallas import tpu as pltpu

F32 = jnp.float32
BF16 = jnp.bfloat16
HIGHEST = lax.Precision.HIGHEST

HEAD = 64
GRID_W = 64
CONV_K = 3
TOP_K = 2
LN_EPS = 1e-5
GN_EPS = 64e-5
NORM_EPS = 1e-12
PAD_LOGIT = -1e30
ROUTE_E1, ROUTE_E2, ROUTE_W1, ROUTE_W2 = 0, 1, 2, 3
MOE_ROW_TILE = 448
DMA_LOOP_UNROLL = 8
MXU_WIDTH_V7X = 256
GROUP = MXU_WIDTH_V7X // HEAD
CHUNK = MXU_WIDTH_V7X // GROUP
SCAN_BATCH_BLOCK = 4
TOK_BLOCK = 256
LANE = 128
VMEM_LIMIT_V7X = 56 * 1024 * 1024
FFN_M_BLOCKS_PER_BATCH = 4
FFN_F_BLOCK = 512
MOD_N_BLOCK = 1536


def _mm(a, b):
    return jnp.dot(a.astype(BF16), b.astype(BF16), preferred_element_type=F32)


def _mm_nt(a, b):
    return lax.dot_general(a.astype(BF16), b.astype(BF16), (((1,), (1,)), ((), ())),
                           preferred_element_type=F32)


def _sigmoid(x):
    return 1.0 / (1.0 + jnp.exp(-x))


def _layer_norm(z, g, b):
    mu = jnp.mean(z, axis=-1, keepdims=True)
    d = z - mu
    var = jnp.mean(d * d, axis=-1, keepdims=True)
    return d * lax.rsqrt(var + LN_EPS) * g + b


def _params(sem):
    return pltpu.CompilerParams(dimension_semantics=sem, vmem_limit_bytes=VMEM_LIMIT_V7X)


def _mod_kernel(c_ref, w_ref, b_ref, o_ref):
    c = c_ref[...]
    s = c * _sigmoid(c)
    o_ref[...] = jnp.dot(s, w_ref[...], precision=HIGHEST, preferred_element_type=F32) + b_ref[...]


def _modulation(cc, w_mod, b_mod):
    depth, d, n = w_mod.shape
    tn = MOD_N_BLOCK
    return pl.pallas_call(
        _mod_kernel,
        grid=(depth, n // tn),
        in_specs=[pl.BlockSpec((8, d), lambda l, j: (0, 0)),
                  pl.BlockSpec((None, d, tn), lambda l, j: (l, 0, j)),
                  pl.BlockSpec((None, 1, tn), lambda l, j: (l, 0, j))],
        out_specs=pl.BlockSpec((None, 8, tn), lambda l, j: (l, 0, j)),
        out_shape=jax.ShapeDtypeStruct((depth, 8, n), F32),
        compiler_params=_params(("parallel", "parallel")),
        name="adaln_modulation",
    )(cc, w_mod, b_mod.reshape(depth, 1, n))


def _inproj_kernel(x_ref, mod_ref, w_ref, w2s_ref, a2s_ref, w0_ref, a0_ref, kk_ref, ka_ref, rk_ref, g2_ref,
                   cw_ref, ones_ref, sh_ref, dir_ref, ex_ref, *, width, blocks_per_batch):
    wd_ = width
    x = x_ref[...]
    h = x * (1.0 + mod_ref[1:2, :]) + mod_ref[0:1, :]
    p = _mm(h, w_ref[...])
    r = p[:, 0:wd_]
    k = p[:, wd_:2 * wd_]
    v = p[:, 2 * wd_:3 * wd_]
    o = 3 * wd_
    wd = p[:, o:o + LANE]
    ad = p[:, o + LANE:o + 2 * LANE]
    gd = p[:, o + 2 * LANE:o + 3 * LANE]
    o = o + 3 * LANE
    cb = p[:, o:o + wd_]
    cc = p[:, o + wd_:o + 2 * wd_]
    ch = p[:, o + 2 * wd_:o + 3 * wd_]
    ones = ones_ref[...]

    z = w0_ref[...] + _mm(jnp.tanh(wd), w2s_ref[...])
    softplus_neg = jnp.maximum(-z, 0.0) + jnp.log(1.0 + jnp.exp(-jnp.abs(z)))
    w_log = -softplus_neg - 0.5
    log_decay = -jnp.exp(w_log)
    a_all = _sigmoid(a0_ref[...] + _mm(ad, a2s_ref[...]))

    kk_raw = k * kk_ref[...]
    ss = _mm(kk_raw * kk_raw, ones)
    kk = kk_raw / jnp.maximum(jnp.sqrt(ss), NORM_EPS)

    sh_ref[:, 0:wd_] = r
    sh_ref[:, wd_:2 * wd_] = kk
    sh_ref[:, 2 * wd_:3 * wd_] = v
    kd_sum = jnp.zeros_like(k)
    for d in range(2):
        a = a_all[:, d * wd_:(d + 1) * wd_]
        kd = k * (1.0 + (a - 1.0) * ka_ref[...])
        kd_sum = kd_sum + kd
        dir_ref[d, :, 0:wd_] = log_decay[:, d * wd_:(d + 1) * wd_]
        dir_ref[d, :, wd_:2 * wd_] = kd
        dir_ref[d, :, 2 * wd_:3 * wd_] = kk * a
    bonus = _mm(r * kd_sum * rk_ref[...], ones) * v
    gate = _mm(_sigmoid(gd), g2_ref[...])

    u = cc * ch
    rows = lax.broadcasted_iota(jnp.int32, u.shape, 0)
    is_ctx = (pl.program_id(0) % blocks_per_batch) == 0
    period_mask = jnp.where(is_ctx, TOK_BLOCK - 1, GRID_W - 1)
    pos = rows & period_mask
    u_prev = jnp.where(pos == 0, 0.0, pltpu.roll(u, 1, 0))
    u_next = jnp.where(pos == period_mask, 0.0, pltpu.roll(u, TOK_BLOCK - 1, 0))
    conv = cb * (u_prev * cw_ref[0:1, :] + u * cw_ref[1:2, :] + u_next * cw_ref[2:3, :])

    ex_ref[:, 0:wd_] = bonus
    ex_ref[:, wd_:2 * wd_] = gate
    ex_ref[:, 2 * wd_:3 * wd_] = conv


def _inproj(x, modtab, w_in, w2s, a2s, w0, a0, k_k, k_a, r_k, g2, conv_w, ones, *, blocks_per_batch):
    ntok, d = x.shape
    width = k_k.shape[-1]
    pcols = w_in.shape[1]
    nb = ntok // TOK_BLOCK
    const = lambda *shape: pl.BlockSpec(shape, lambda i: (0,) * len(shape))
    return pl.pallas_call(
        functools.partial(_inproj_kernel, width=width, blocks_per_batch=blocks_per_batch),
        grid=(nb,),
        in_specs=[pl.BlockSpec((TOK_BLOCK, d), lambda i: (i, 0)),
                  pl.BlockSpec((None, None, 6, d),
                               lambda i: (i // blocks_per_batch, jnp.minimum(i % blocks_per_batch, 1), 0, 0)),
                  const(d, pcols), const(LANE, 2 * width), const(LANE, 2 * width), const(1, 2 * width),
                  const(1, 2 * width), const(1, width), const(1, width), const(1, width), const(LANE, width),
                  const(CONV_K, width), const(width, width)],
        out_specs=[pl.BlockSpec((TOK_BLOCK, 3 * width), lambda i: (i, 0)),
                   pl.BlockSpec((2, TOK_BLOCK, 3 * width), lambda i: (0, i, 0)),
                   pl.BlockSpec((TOK_BLOCK, 3 * width), lambda i: (i, 0))],
        out_shape=[jax.ShapeDtypeStruct((ntok, 3 * width), F32),
                   jax.ShapeDtypeStruct((2, ntok, 3 * width), F32),
                   jax.ShapeDtypeStruct((ntok, 3 * width), F32)],
        compiler_params=_params(("parallel",)),
        name="inproj_prep",
    )(x, modtab, w_in, w2s, a2s, w0, a0, k_k, k_a, r_k, g2, conv_w, ones)


def _scan_masks(rev):
    n = GROUP * CHUNK
    ri = lax.broadcasted_iota(jnp.int32, (n, n), 0)
    ci = lax.broadcasted_iota(jnp.int32, (n, n), 1)
    same_head = (ri // HEAD) == (ci // HEAD)
    t = lax.broadcasted_iota(jnp.int32, (CHUNK, n), 0)
    s = lax.broadcasted_iota(jnp.int32, (CHUNK, n), 1) % CHUNK
    before = (s > t) if rev else (s < t)
    levels = []
    size = 1
    while size < CHUNK:
        levels.append(((t // (2 * size)) == (s // (2 * size))) & ((t // size) != (s // size)))
        size *= 2
    return same_head, before, before | (s == t), s == t, levels


def _scan_chunks(streams):
    n = GROUP * CHUNK
    ti = lax.broadcasted_iota(jnp.int32, (CHUNK, CHUNK), 0)
    si = lax.broadcasted_iota(jnp.int32, (CHUNK, CHUNK), 1)

    def blk(m, same_head):
        return jnp.where(same_head, jnp.concatenate([m] * GROUP, axis=0), 0.0).astype(BF16)

    def prepare(r, kk, v, lw, kd, b, t_prev, masks, rev):
        same_head = masks[0]
        tri = jnp.where((si >= ti) if rev else (si <= ti), 1.0, 0.0).astype(BF16)
        lw_hi = lw.astype(BF16)
        lw_rest = lw - lw_hi.astype(F32)
        lw_mid = lw_rest.astype(BF16)
        lw_lo = (lw_rest - lw_mid.astype(F32)).astype(BF16)
        g_parts = jnp.dot(tri, jnp.concatenate([lw_hi, lw_mid, lw_lo], axis=1), preferred_element_type=F32)
        g_inc = g_parts[:, 0:n] + g_parts[:, n:2 * n] + g_parts[:, 2 * n:3 * n]
        g_exc = g_inc - lw
        g_end = g_inc[0:1, :] if rev else g_inc[CHUNK - 1:CHUNK, :]
        e_neg = jnp.exp(-g_inc)
        e_end = jnp.exp(g_end - g_inc)
        decay = jnp.exp(jnp.broadcast_to(g_end, (LANE, n)).T)
        return dict(kk=(kk * jnp.exp(g_exc)).astype(BF16), r=(r * jnp.exp(g_inc)).astype(BF16),
                    bk=jnp.concatenate([blk(b * e_neg, same_head), blk(kd * e_neg, same_head)], axis=0),
                    ends_t=jnp.concatenate([b * e_end, kd * e_end], axis=0).T.astype(BF16),
                    v=v.astype(BF16), v_blk=blk(v, same_head),
                    decay=jnp.concatenate([decay] * (n // LANE), axis=1), t=t_prev, t_b=t_prev.astype(BF16))

    ops = [prepare(*st) for st in streams]
    masks = [st[7] for st in streams]
    kr = [jnp.concatenate([o["kk"], o["r"]], axis=0) for o in ops]
    pairs = [_mm_nt(a, o["bk"]) for a, o in zip(kr, ops)]
    on_t = [_mm(a, o["t_b"]) for a, o in zip(kr, ops)]
    l_b = [jnp.where(m[1], p[0:CHUNK, 0:n], 0.0) for p, m in zip(pairs, masks)]
    a_rb = [jnp.where(m[2], p[CHUNK:2 * CHUNK, 0:n], 0.0).astype(BF16) for p, m in zip(pairs, masks)]
    on_v = [_mm(jnp.where(jnp.concatenate([m[1], m[2]], axis=0), p[:, n:2 * n], 0.0), o["v_blk"])
            for p, m, o in zip(pairs, masks, ops)]

    inv = [jnp.where(m[3], 1.0, 0.0) - jnp.where(m[4][0], lb, 0.0) for lb, m in zip(l_b, masks)]
    for lvl in range(1, len(masks[0][4])):
        half = [_mm(jnp.where(m[4][lvl], lb, 0.0), blk(a, m[0])) for lb, a, m in zip(l_b, inv, masks)]
        inv = [a - _mm(a, blk(h, m[0])) for a, h, m in zip(inv, half, masks)]

    x = [-(ot[0:CHUNK] + ov[0:CHUNK]) for ot, ov in zip(on_t, on_v)]
    u = [_mm(a, blk(xx, m[0])) for a, xx, m in zip(inv, x, masks)]
    ys = [ot[CHUNK:2 * CHUNK] + ov[CHUNK:2 * CHUNK] + _mm(ar, blk(uu, m[0]))
          for ot, ov, ar, uu, m in zip(on_t, on_v, a_rb, u, masks)]
    t_new = [o["decay"] * o["t"]
             + jnp.where(m[0], _mm(o["ends_t"], jnp.concatenate([uu.astype(BF16), o["v"]], axis=0)), 0.0)
             for o, uu, m in zip(ops, u, masks)]
    return ys, t_new


def _scan_kernel(shf_ref, shb_ref, df_ref, db_ref, yf_ref, yb_ref, s_ref, *, width):
    @pl.when(pl.program_id(1) == 0)
    def _():
        s_ref[...] = jnp.zeros_like(s_ref)

    gw = GROUP * HEAD
    n_groups = width // gw
    streams, sinks = [], []
    for d, (sh, dr, yo) in enumerate(((shf_ref, df_ref, yf_ref), (shb_ref, db_ref, yb_ref))):
        rev = d == 1
        masks = _scan_masks(rev)
        for j in range(sh.shape[0]):
            for g in range(n_groups):
                lo = g * gw
                idx = (d * sh.shape[0] + j) * n_groups + g
                streams.append((sh[j, :, lo:lo + gw], sh[j, :, width + lo:width + lo + gw],
                                sh[j, :, 2 * width + lo:2 * width + lo + gw],
                                dr[j, :, lo:lo + gw], dr[j, :, width + lo:width + lo + gw],
                                dr[j, :, 2 * width + lo:2 * width + lo + gw],
                                s_ref[idx], masks, rev))
                sinks.append((yo, j, lo, idx))
    ys, s_new = _scan_chunks(streams)
    for (yo, j, lo, idx), y, s in zip(sinks, ys, s_new):
        yo[j, :, lo:lo + gw] = y
        s_ref[idx] = s


def _wkv_scan(shared, per_dir, *, batch, ctx_chunks):
    ntok, w3 = shared.shape
    width = w3 // 3
    tok = ntok // batch
    nc = tok // CHUNK
    n_groups = width // (GROUP * HEAD)
    bb = SCAN_BATCH_BLOCK if batch % SCAN_BATCH_BLOCK == 0 else 1
    shared = shared.reshape(batch, tok, w3)
    per_dir = per_dir.reshape(2, batch, tok, w3)

    def bwd(i):
        return jnp.where(i < ctx_chunks, ctx_chunks - 1 - i, nc - 1 - (i - ctx_chunks))

    yf, yb = pl.pallas_call(
        functools.partial(_scan_kernel, width=width),
        grid=(batch // bb, nc),
        in_specs=[pl.BlockSpec((bb, CHUNK, w3), lambda b, i: (b, i, 0)),
                  pl.BlockSpec((bb, CHUNK, w3), lambda b, i: (b, bwd(i), 0)),
                  pl.BlockSpec((None, bb, CHUNK, w3), lambda b, i: (0, b, i, 0)),
                  pl.BlockSpec((None, bb, CHUNK, w3), lambda b, i: (1, b, bwd(i), 0))],
        out_specs=[pl.BlockSpec((bb, CHUNK, width), lambda b, i: (b, i, 0)),
                   pl.BlockSpec((bb, CHUNK, width), lambda b, i: (b, bwd(i), 0))],
        out_shape=[jax.ShapeDtypeStruct((batch, tok, width), F32)] * 2,
        scratch_shapes=[pltpu.VMEM((2 * bb * n_groups, GROUP * HEAD, GROUP * HEAD), F32)],
        compiler_params=_params(("parallel", "arbitrary")),
        name="wkv_scan",
    )(shared, shared, per_dir, per_dir)
    return yf.reshape(ntok, width), yb.reshape(ntok, width)


def _top2_route(logits):
    lane = lax.broadcasted_iota(jnp.int32, logits.shape, 1)
    n = logits.shape[-1]
    m1 = jnp.max(logits, axis=-1, keepdims=True)
    i1 = jnp.min(jnp.where(logits == m1, lane, n), axis=-1, keepdims=True)
    rest = jnp.where(lane == i1, PAD_LOGIT, logits)
    m2 = jnp.max(rest, axis=-1, keepdims=True)
    i2 = jnp.min(jnp.where(rest == m2, lane, n), axis=-1, keepdims=True)
    e = jnp.exp(m2 - m1)
    rec = jnp.where(lane == ROUTE_W2, e / (1.0 + e), 0.0)
    rec = jnp.where(lane == ROUTE_W1, 1.0 / (1.0 + e), rec)
    rec = jnp.where(lane == ROUTE_E2, i2.astype(F32), rec)
    return jnp.where(lane == ROUTE_E1, i1.astype(F32), rec)


def _outproj_kernel(*refs, width, alpha, moe):
    if moe:
        (yf_ref, yb_ref, ex_ref, x_ref, mod_ref, lg_ref, lb_ref, wo_ref, g1_ref, b1_ref, ones_ref,
         rw_ref, rb_ref, x1_ref, h2_ref, gate_ref) = refs
    else:
        (yf_ref, yb_ref, ex_ref, x_ref, mod_ref, lg_ref, lb_ref, wo_ref, g1_ref, b1_ref, ones_ref,
         x1_ref, h2_ref) = refs
    ones = ones_ref[...]
    y = yf_ref[...] + yb_ref[...]
    inv_n = 1.0 / HEAD
    mu = _mm(y, ones) * inv_n
    dlt = y - mu
    var = _mm(dlt * dlt, ones) * inv_n
    yn = dlt * lax.rsqrt(var + GN_EPS)
    bonus = ex_ref[:, 0:width]
    gate = ex_ref[:, width:2 * width]
    conv = ex_ref[:, 2 * width:3 * width]
    rwkv = (yn * lg_ref[...] + lb_ref[...] + bonus) * gate
    cat = jnp.concatenate([rwkv.astype(BF16), conv.astype(BF16)], axis=1)
    mix = jnp.dot(cat, wo_ref[...], preferred_element_type=F32)
    x1 = _layer_norm(alpha * x_ref[...] + mod_ref[2:3, :] * mix, g1_ref[...], b1_ref[...])
    h2 = x1 * (1.0 + mod_ref[4:5, :]) + mod_ref[3:4, :]
    x1_ref[...] = x1
    h2_ref[...] = h2.astype(h2_ref.dtype)
    if moe:
        h_hi = h2.astype(BF16)
        h_lo = (h2 - h_hi.astype(F32)).astype(BF16)
        rw = rw_ref[...]
        w_hi = rw.astype(BF16)
        w_lo = (rw - w_hi.astype(F32)).astype(BF16)
        logits = jnp.dot(jnp.concatenate([h_hi, h_hi, h_lo], axis=1), jnp.concatenate([w_hi, w_lo, w_hi], axis=0),
                         preferred_element_type=F32) + rb_ref[...]
        gate_ref[...] = _top2_route(logits)


def _outproj(yf, yb, extras, x, modtab, lnx_g, lnx_b, w_out, ln_g, ln_b, ones, router, *, alpha,
             blocks_per_batch):
    ntok, d = x.shape
    width = yf.shape[1]
    nb = ntok // TOK_BLOCK
    moe = router is not None
    const = lambda *shape: pl.BlockSpec(shape, lambda i: (0,) * len(shape))
    row = lambda cols: pl.BlockSpec((TOK_BLOCK, cols), lambda i: (i, 0))
    in_specs = [row(width), row(width), row(3 * width), row(d),
                pl.BlockSpec((None, None, 6, d),
                             lambda i: (i // blocks_per_batch, jnp.minimum(i % blocks_per_batch, 1), 0, 0)),
                const(1, width), const(1, width), const(2 * width, d), const(1, d), const(1, d),
                const(width, width)]
    out_specs = [row(d), row(d)]
    out_shape = [jax.ShapeDtypeStruct((ntok, d), F32), jax.ShapeDtypeStruct((ntok, d), F32 if moe else BF16)]
    args = [yf, yb, extras, x, modtab, lnx_g, lnx_b, w_out, ln_g, ln_b, ones]
    if moe:
        in_specs += [const(d, LANE), const(1, LANE)]
        out_specs.append(row(LANE))
        out_shape.append(jax.ShapeDtypeStruct((ntok, LANE), F32))
        args += list(router)
    return pl.pallas_call(
        functools.partial(_outproj_kernel, width=width, alpha=alpha, moe=moe),
        grid=(nb,), in_specs=in_specs, out_specs=out_specs, out_shape=out_shape,
        compiler_params=_params(("parallel",)),
        name="readout_outproj",
    )(*args)


def _swiglu_tile(h, w1_ref, w3_ref, w2_ref):
    a = jnp.dot(h, w1_ref[...], preferred_element_type=F32)
    b = jnp.dot(h, w3_ref[...], preferred_element_type=F32)
    s = a * _sigmoid(a) * b
    return jnp.dot(s.astype(BF16), w2_ref[...], preferred_element_type=F32)


def _ffn_kernel(h_ref, x1_ref, w1_ref, w3_ref, w2_ref, mod_ref, g_ref, b_ref, o_ref, acc_ref, *, alpha, ctx_len,
                m_blocks_per_batch):
    f = pl.program_id(1)

    @pl.when(f == 0)
    def _():
        acc_ref[...] = jnp.zeros_like(acc_ref)

    acc_ref[...] += _swiglu_tile(h_ref[...], w1_ref, w3_ref, w2_ref)

    @pl.when(f == pl.num_programs(1) - 1)
    def _():
        tm = acc_ref.shape[0]
        rows = lax.broadcasted_iota(jnp.int32, acc_ref.shape, 0)
        row_in_batch = rows + (pl.program_id(0) % m_blocks_per_batch) * tm
        gate2 = jnp.where(row_in_batch < ctx_len, mod_ref[0, 5:6, :], mod_ref[1, 5:6, :])
        o_ref[...] = _layer_norm(alpha * x1_ref[...] + gate2 * acc_ref[...], g_ref[...], b_ref[...])


def _ffn(h2, x1, w1, w3, w2, layer, modtab, ln_g, ln_b, *, alpha, ctx_len, batch):
    ntok, d = x1.shape
    dff = w1.shape[2]
    mpb = FFN_M_BLOCKS_PER_BATCH
    tm = ntok // batch // mpb
    tf = min(FFN_F_BLOCK, dff)
    row = lambda cols: pl.BlockSpec((tm, cols), lambda m, f: (m, 0))
    const = lambda *shape: pl.BlockSpec(shape, lambda m, f: (0,) * len(shape))
    in_specs = [row(d), row(d),
                pl.BlockSpec((None, d, tf), lambda m, f: (layer, 0, f)),
                pl.BlockSpec((None, d, tf), lambda m, f: (layer, 0, f)),
                pl.BlockSpec((None, tf, d), lambda m, f: (layer, f, 0)),
                pl.BlockSpec((None, 2, 6, d), lambda m, f: (m // mpb, 0, 0, 0)),
                const(1, d), const(1, d)]
    return pl.pallas_call(
        functools.partial(_ffn_kernel, alpha=alpha, ctx_len=ctx_len, m_blocks_per_batch=mpb),
        grid=(ntok // tm, dff // tf),
        in_specs=in_specs, out_specs=row(d),
        out_shape=jax.ShapeDtypeStruct((ntok, d), F32),
        scratch_shapes=[pltpu.VMEM((tm, d), F32)],
        compiler_params=_params(("parallel", "arbitrary")),
        name="dense_swiglu",
    )(h2, x1, w1, w3, w2, modtab, ln_g, ln_b)


def _route_plan(route, n_exp):
    ntok = route.shape[0]
    tm = MOE_ROW_TILE
    n_tiles_max = -(-(TOP_K * ntok + n_exp * (tm - 1)) // tm)
    expert = jnp.concatenate([route[:, ROUTE_E1], route[:, ROUTE_E2]]).astype(jnp.int32)
    onehot = (expert[:, None] == jnp.arange(n_exp, dtype=jnp.int32)[None, :]).astype(jnp.int32)
    running = jnp.cumsum(onehot, axis=0)
    rank = jnp.sum(running * onehot, axis=1) - 1
    tiles = (running[-1] + tm - 1) // tm
    tile_end = jnp.cumsum(tiles)
    slot = (tile_end - tiles)[expert] * tm + rank
    n_tiles = tile_end[-1]
    tile_ids = jnp.minimum(jnp.arange(n_tiles_max, dtype=jnp.int32), n_tiles - 1)
    tile_expert = jnp.sum((tile_ids[:, None] >= tile_end[None, :]).astype(jnp.int32), axis=1)
    token = jnp.arange(TOP_K * ntok, dtype=jnp.int32) % ntok
    slot_token = jnp.zeros((n_tiles_max * tm,), jnp.int32).at[slot].set(token)
    return tile_expert, n_tiles.reshape(1), slot_token, slot


def _moe_experts_kernel(te_ref, nt_ref, tok_ref, h_hbm, w1_ref, w3_ref, w2_ref, o_ref, rows_ref, hb_ref, acc_ref,
                        sems, *, nf):
    t = pl.program_id(0)
    f = pl.program_id(1)
    tm = hb_ref.shape[0]
    per_step = tm // nf
    n_used = nt_ref[0]
    used = t < n_used
    cur = t % 2

    def row_copy(buf, r, src_row):
        return pltpu.make_async_copy(h_hbm.at[pl.ds(src_row, 1)], rows_ref.at[buf, pl.ds(r, 1)], sems.at[buf])

    @pl.when((t == 0) & (f == 0))
    def _():
        def start(r, carry):
            row_copy(0, r, tok_ref[r]).start()
            return carry

        lax.fori_loop(0, tm, start, 0, unroll=DMA_LOOP_UNROLL)

    @pl.when(used & (f == 0))
    def _():
        def wait(r, carry):
            row_copy(cur, r, 0).wait()
            return carry

        lax.fori_loop(0, tm, wait, 0, unroll=DMA_LOOP_UNROLL)
        hb_ref[...] = rows_ref[cur].astype(BF16)
        acc_ref[...] = jnp.zeros_like(acc_ref)

    has_next = t + 1 < n_used

    @pl.when(used & has_next)
    def _():
        base = f * per_step
        for j in range(per_step):
            row_copy(1 - cur, base + j, tok_ref[(t + 1) * tm + base + j]).start()
        acc_ref[...] += _swiglu_tile(hb_ref[...], w1_ref, w3_ref, w2_ref)

    @pl.when(used & jnp.logical_not(has_next))
    def _():
        acc_ref[...] += _swiglu_tile(hb_ref[...], w1_ref, w3_ref, w2_ref)

    @pl.when(f == pl.num_programs(1) - 1)
    def _():
        @pl.when(used)
        def _():
            o_ref[...] = acc_ref[...]

        @pl.when(jnp.logical_not(used))
        def _():
            o_ref[...] = jnp.zeros_like(o_ref)


def _moe_experts(h2, plan, w1, w3, w2, layer):
    tile_expert, n_tiles, slot_token, _ = plan
    ntok, d = h2.shape
    dff = w1.shape[3]
    tm = MOE_ROW_TILE
    tf = min(FFN_F_BLOCK, dff)
    nf = dff // tf
    assert tm % nf == 0
    f_of = lambda t, f, nt: jnp.where(t < nt[0], f, nf - 1)
    grid_spec = pltpu.PrefetchScalarGridSpec(
        num_scalar_prefetch=3,
        grid=(tile_expert.shape[0], nf),
        in_specs=[pl.BlockSpec(memory_space=pl.ANY),
                  pl.BlockSpec((None, None, d, tf), lambda t, f, te, nt, tk: (layer, te[t], 0, f_of(t, f, nt))),
                  pl.BlockSpec((None, None, d, tf), lambda t, f, te, nt, tk: (layer, te[t], 0, f_of(t, f, nt))),
                  pl.BlockSpec((None, None, tf, d), lambda t, f, te, nt, tk: (layer, te[t], f_of(t, f, nt), 0))],
        out_specs=pl.BlockSpec((tm, d), lambda t, f, te, nt, tk: (t, 0)),
        scratch_shapes=[pltpu.VMEM((2, tm, d), F32), pltpu.VMEM((tm, d), BF16), pltpu.VMEM((tm, d), F32),
                        pltpu.SemaphoreType.DMA((2,))])
    return pl.pallas_call(
        functools.partial(_moe_experts_kernel, nf=nf), grid_spec=grid_spec,
        out_shape=jax.ShapeDtypeStruct((slot_token.shape[0], d), F32),
        compiler_params=_params(("arbitrary", "arbitrary")),
        name="moe_experts",
    )(tile_expert, n_tiles, slot_token, h2, w1, w3, w2)


def _moe_combine_kernel(slot_ref, y_hbm, x1_ref, route_ref, mod_ref, g_ref, b_ref, o_ref, rows_ref, sems, *, alpha,
                        ntok):
    i = pl.program_id(0)
    tb = x1_ref.shape[0]
    cur = i % 2

    def row_copy(buf, k, r, src_row):
        return pltpu.make_async_copy(y_hbm.at[pl.ds(src_row, 1)], rows_ref.at[buf, k, pl.ds(r, 1)], sems.at[buf])

    def start_block(block, buf):
        def start(r, carry):
            for k in range(TOP_K):
                row_copy(buf, k, r, slot_ref[k * ntok + block * tb + r]).start()
            return carry

        lax.fori_loop(0, tb, start, 0, unroll=DMA_LOOP_UNROLL)

    @pl.when(i == 0)
    def _():
        start_block(0, 0)

    @pl.when(i + 1 < pl.num_programs(0))
    def _():
        start_block(i + 1, 1 - cur)

    def wait(r, carry):
        for k in range(TOP_K):
            row_copy(cur, k, r, 0).wait()
        return carry

    lax.fori_loop(0, tb, wait, 0, unroll=DMA_LOOP_UNROLL)
    mix = (route_ref[:, ROUTE_W1:ROUTE_W1 + 1] * rows_ref[cur, 0]
           + route_ref[:, ROUTE_W2:ROUTE_W2 + 1] * rows_ref[cur, 1])
    o_ref[...] = _layer_norm(alpha * x1_ref[...] + mod_ref[5:6, :] * mix, g_ref[...], b_ref[...])


def _moe_combine(y_slots, plan, x1, route, modtab, ln_g, ln_b, *, alpha, blocks_per_batch):
    ntok, d = x1.shape
    slot = plan[3]
    row = lambda cols: pl.BlockSpec((TOK_BLOCK, cols), lambda i, s: (i, 0))
    const = lambda *shape: pl.BlockSpec(shape, lambda i, s: (0,) * len(shape))
    grid_spec = pltpu.PrefetchScalarGridSpec(
        num_scalar_prefetch=1,
        grid=(ntok // TOK_BLOCK,),
        in_specs=[pl.BlockSpec(memory_space=pl.ANY), row(d), row(LANE),
                  pl.BlockSpec((None, None, 6, d),
                               lambda i, s: (i // blocks_per_batch, jnp.minimum(i % blocks_per_batch, 1), 0, 0)),
                  const(1, d), const(1, d)],
        out_specs=row(d),
        scratch_shapes=[pltpu.VMEM((2, TOP_K, TOK_BLOCK, d), F32), pltpu.SemaphoreType.DMA((2,))])
    return pl.pallas_call(
        functools.partial(_moe_combine_kernel, alpha=alpha, ntok=ntok), grid_spec=grid_spec,
        out_shape=jax.ShapeDtypeStruct((ntok, d), F32),
        compiler_params=_params(("arbitrary",)),
        name="moe_combine",
    )(slot, y_slots, x1, route, modtab, ln_g, ln_b)


def _head_ones(width):
    hid = jnp.arange(width) // HEAD
    return (hid[:, None] == hid[None, :]).astype(BF16)


def _stack_lora(w):
    _, rank, width = w.shape
    z = jnp.zeros((rank, width), w.dtype)
    return jnp.concatenate([jnp.concatenate([w[0], z], axis=1), jnp.concatenate([z, w[1]], axis=1)], axis=0)


def kernel(x, c, ctx, c_ctx, w_mod, b_mod, w_in, rk_w0, rk_w2, rk_a0, rk_a2, rk_kk, rk_ka, rk_rk, rk_g2,
           rk_lnx_g, rk_lnx_b, conv_w, w_out, ln1_g, ln1_b, ln2_g, ln2_b, ffn_w1, ffn_w3, ffn_w2,
           router_w, router_b, moe_w1, moe_w3, moe_w2):
    batch, seq, d = x.shape
    ctx_len = ctx.shape[1]
    depth = w_in.shape[0]
    width = rk_kk.shape[-1]
    n_exp = router_w.shape[-1]
    tok = ctx_len + seq
    assert ctx_len == TOK_BLOCK and seq % TOK_BLOCK == 0 and seq % GRID_W == 0 and batch < 8
    assert 2 * rk_w2.shape[2] == LANE and 2 * rk_a2.shape[2] == LANE and rk_g2.shape[1] == LANE
    assert width % (GROUP * HEAD) == 0 and n_exp <= LANE and tok % (FFN_M_BLOCKS_PER_BATCH * 16) == 0
    blocks_per_batch = tok // TOK_BLOCK
    alpha = (2 * depth) ** 0.25

    cc = jnp.concatenate([c, c_ctx[None, :], jnp.zeros((8 - batch - 1, d), F32)], axis=0)
    mods = _modulation(cc, w_mod, b_mod).reshape(depth, 8, 6, d)
    modtab = jnp.stack([jnp.broadcast_to(mods[:, batch][:, None], (depth, batch, 6, d)), mods[:, :batch]],
                       axis=2)

    xs = jnp.concatenate([ctx, x], axis=1).reshape(batch * tok, d)
    ones = _head_ones(width)
    ffn_w = [w.astype(BF16) for w in (ffn_w1, ffn_w3, ffn_w2)]
    moe_w = [w.astype(BF16) for w in (moe_w1, moe_w3, moe_w2)]
    for l in range(depth):
        shared, per_dir, extras = _inproj(
            xs, modtab[l], w_in[l].astype(BF16), _stack_lora(rk_w2[l]).astype(BF16),
            _stack_lora(rk_a2[l]).astype(BF16), rk_w0[l].reshape(1, 2 * width), rk_a0[l].reshape(1, 2 * width),
            rk_kk[l][None], rk_ka[l][None], rk_rk[l].reshape(1, width), rk_g2[l].astype(BF16), conv_w[l], ones,
            blocks_per_batch=blocks_per_batch)
        yf, yb = _wkv_scan(shared, per_dir, batch=batch, ctx_chunks=ctx_len // CHUNK)
        i = l // 2
        router = None
        if l % 2 == 1:
            router = (jnp.pad(router_w[i], ((0, 0), (0, LANE - n_exp))),
                      jnp.pad(router_b[i], (0, LANE - n_exp), constant_values=PAD_LOGIT)[None])
        res = _outproj(yf, yb, extras, xs, modtab[l], rk_lnx_g[l][None], rk_lnx_b[l][None],
                       w_out[l].astype(BF16), ln1_g[l][None], ln1_b[l][None], ones, router,
                       alpha=alpha, blocks_per_batch=blocks_per_batch)
        if l % 2 == 0:
            x1, h2 = res
            xs = _ffn(h2, x1, *ffn_w, i, modtab[l], ln2_g[l][None], ln2_b[l][None], alpha=alpha, ctx_len=ctx_len,
                      batch=batch)
        else:
            x1, h2, route = res
            plan = _route_plan(route, n_exp)
            y_slots = _moe_experts(h2, plan, *moe_w, i)
            xs = _moe_combine(y_slots, plan, x1, route, modtab[l], ln2_g[l][None], ln2_b[l][None],
                              alpha=alpha, blocks_per_batch=blocks_per_batch)
    return xs.reshape(batch, tok, d)[:, ctx_len:]
```

```python
import functools

import jax
import jax.numpy as jnp
from jax import lax
from jax.experimental import pallas as pl
from jax.experimental.pallas import tpu as pltpu

F32 = jnp.float32
BF16 = jnp.bfloat16
HIGHEST = lax.Precision.HIGHEST

HEAD = 64
GRID_W = 64
CONV_K = 3
TOP_K = 2
LN_EPS = 1e-5
GN_EPS = 64e-5
NORM_EPS = 1e-12
PAD_LOGIT = -1e30
ROUTE_E1, ROUTE_E2, ROUTE_W1, ROUTE_W2 = 0, 1, 2, 3
MOE_ROW_TILE = 448
DMA_LOOP_UNROLL = 8
MXU_WIDTH_V7X = 256
GROUP = MXU_WIDTH_V7X // HEAD
CHUNK = MXU_WIDTH_V7X // GROUP
SCAN_BATCH_BLOCK = 4
TOK_BLOCK = 256
LANE = 128
VMEM_LIMIT_V7X = 56 * 1024 * 1024
FFN_M_BLOCKS_PER_BATCH = 4
FFN_F_BLOCK = 512
MOD_N_BLOCK = 1536


def _mm(a, b):
    return jnp.dot(a.astype(BF16), b.astype(BF16), preferred_element_type=F32)


def _mm_nt(a, b):
    return lax.dot_general(a.astype(BF16), b.astype(BF16), (((1,), (1,)), ((), ())),
                           preferred_element_type=F32)


def _sigmoid(x):
    return 1.0 / (1.0 + jnp.exp(-x))


def _layer_norm(z, g, b):
    mu = jnp.mean(z, axis=-1, keepdims=True)
    d = z - mu
    var = jnp.mean(d * d, axis=-1, keepdims=True)
    return d * lax.rsqrt(var + LN_EPS) * g + b


def _params(sem):
    return pltpu.CompilerParams(dimension_semantics=sem, vmem_limit_bytes=VMEM_LIMIT_V7X)


def _mod_kernel(c_ref, w_ref, b_ref, o_ref):
    c = c_ref[...]
    s = c * _sigmoid(c)
    o_ref[...] = jnp.dot(s, w_ref[...], precision=HIGHEST, preferred_element_type=F32) + b_ref[...]


def _modulation(cc, w_mod, b_mod):
    depth, d, n = w_mod.shape
    tn = MOD_N_BLOCK
    return pl.pallas_call(
        _mod_kernel,
        grid=(depth, n // tn),
        in_specs=[pl.BlockSpec((8, d), lambda l, j: (0, 0)),
                  pl.BlockSpec((None, d, tn), lambda l, j: (l, 0, j)),
                  pl.BlockSpec((None, 1, tn), lambda l, j: (l, 0, j))],
        out_specs=pl.BlockSpec((None, 8, tn), lambda l, j: (l, 0, j)),
        out_shape=jax.ShapeDtypeStruct((depth, 8, n), F32),
        compiler_params=_params(("parallel", "parallel")),
        name="adaln_modulation",
    )(cc, w_mod, b_mod.reshape(depth, 1, n))


def _inproj_kernel(x_ref, mod_ref, w_ref, w2s_ref, a2s_ref, w0_ref, a0_ref, kk_ref, ka_ref, rk_ref, g2_ref,
                   cw_ref, ones_ref, sh_ref, dir_ref, ex_ref, *, width, blocks_per_batch):
    wd_ = width
    x = x_ref[...]
    h = x * (1.0 + mod_ref[1:2, :]) + mod_ref[0:1, :]
    p = _mm(h, w_ref[...])
    r = p[:, 0:wd_]
    k = p[:, wd_:2 * wd_]
    v = p[:, 2 * wd_:3 * wd_]
    o = 3 * wd_
    wd = p[:, o:o + LANE]
    ad = p[:, o + LANE:o + 2 * LANE]
    gd = p[:, o + 2 * LANE:o + 3 * LANE]
    o = o + 3 * LANE
    cb = p[:, o:o + wd_]
    cc = p[:, o + wd_:o + 2 * wd_]
    ch = p[:, o + 2 * wd_:o + 3 * wd_]
    ones = ones_ref[...]

    z = w0_ref[...] + _mm(jnp.tanh(wd), w2s_ref[...])
    softplus_neg = jnp.maximum(-z, 0.0) + jnp.log(1.0 + jnp.exp(-jnp.abs(z)))
    w_log = -softplus_neg - 0.5
    log_decay = -jnp.exp(w_log)
    a_all = _sigmoid(a0_ref[...] + _mm(ad, a2s_ref[...]))

    kk_raw = k * kk_ref[...]
    ss = _mm(kk_raw * kk_raw, ones)
    kk = kk_raw / jnp.maximum(jnp.sqrt(ss), NORM_EPS)

    sh_ref[:, 0:wd_] = r
    sh_ref[:, wd_:2 * wd_] = kk
    sh_ref[:, 2 * wd_:3 * wd_] = v
    kd_sum = jnp.zeros_like(k)
    for d in range(2):
        a = a_all[:, d * wd_:(d + 1) * wd_]
        kd = k * (1.0 + (a - 1.0) * ka_ref[...])
        kd_sum = kd_sum + kd
        dir_ref[d, :, 0:wd_] = log_decay[:, d * wd_:(d + 1) * wd_]
        dir_ref[d, :, wd_:2 * wd_] = kd
        dir_ref[d, :, 2 * wd_:3 * wd_] = kk * a
    bonus = _mm(r * kd_sum * rk_ref[...], ones) * v
    gate = _mm(_sigmoid(gd), g2_ref[...])

    u = cc * ch
    rows = lax.broadcasted_iota(jnp.int32, u.shape, 0)
    is_ctx = (pl.program_id(0) % blocks_per_batch) == 0
    period_mask = jnp.where(is_ctx, TOK_BLOCK - 1, GRID_W - 1)
    pos = rows & period_mask
    u_prev = jnp.where(pos == 0, 0.0, pltpu.roll(u, 1, 0))
    u_next = jnp.where(pos == period_mask, 0.0, pltpu.roll(u, TOK_BLOCK - 1, 0))
    conv = cb * (u_prev * cw_ref[0:1, :] + u * cw_ref[1:2, :] + u_next * cw_ref[2:3, :])

    ex_ref[:, 0:wd_] = bonus
    ex_ref[:, wd_:2 * wd_] = gate
    ex_ref[:, 2 * wd_:3 * wd_] = conv


def _inproj(x, modtab, w_in, w2s, a2s, w0, a0, k_k, k_a, r_k, g2, conv_w, ones, *, blocks_per_batch):
    ntok, d = x.shape
    width = k_k.shape[-1]
    pcols = w_in.shape[1]
    nb = ntok // TOK_BLOCK
    const = lambda *shape: pl.BlockSpec(shape, lambda i: (0,) * len(shape))
    return pl.pallas_call(
        functools.partial(_inproj_kernel, width=width, blocks_per_batch=blocks_per_batch),
        grid=(nb,),
        in_specs=[pl.BlockSpec((TOK_BLOCK, d), lambda i: (i, 0)),
                  pl.BlockSpec((None, None, 6, d),
                               lambda i: (i // blocks_per_batch, jnp.minimum(i % blocks_per_batch, 1), 0, 0)),
                  const(d, pcols), const(LANE, 2 * width), const(LANE, 2 * width), const(1, 2 * width),
                  const(1, 2 * width), const(1, width), const(1, width), const(1, width), const(LANE, width),
                  const(CONV_K, width), const(width, width)],
        out_specs=[pl.BlockSpec((TOK_BLOCK, 3 * width), lambda i: (i, 0)),
                   pl.BlockSpec((2, TOK_BLOCK, 3 * width), lambda i: (0, i, 0)),
                   pl.BlockSpec((TOK_BLOCK, 3 * width), lambda i: (i, 0))],
        out_shape=[jax.ShapeDtypeStruct((ntok, 3 * width), F32),
                   jax.ShapeDtypeStruct((2, ntok, 3 * width), F32),
                   jax.ShapeDtypeStruct((ntok, 3 * width), F32)],
        compiler_params=_params(("parallel",)),
        name="inproj_prep",
    )(x, modtab, w_in, w2s, a2s, w0, a0, k_k, k_a, r_k, g2, conv_w, ones)


def _scan_masks(rev):
    n = GROUP * CHUNK
    ri = lax.broadcasted_iota(jnp.int32, (n, n), 0)
    ci = lax.broadcasted_iota(jnp.int32, (n, n), 1)
    same_head = (ri // HEAD) == (ci // HEAD)
    t = lax.broadcasted_iota(jnp.int32, (CHUNK, n), 0)
    s = lax.broadcasted_iota(jnp.int32, (CHUNK, n), 1) % CHUNK
    before = (s > t) if rev else (s < t)
    levels = []
    size = 1
    while size < CHUNK:
        levels.append(((t // (2 * size)) == (s // (2 * size))) & ((t // size) != (s // size)))
        size *= 2
    return same_head, before, before | (s == t), s == t, levels


def _scan_chunks(streams):
    n = GROUP * CHUNK
    ti = lax.broadcasted_iota(jnp.int32, (CHUNK, CHUNK), 0)
    si = lax.broadcasted_iota(jnp.int32, (CHUNK, CHUNK), 1)

    def blk(m, same_head):
        return jnp.where(same_head, jnp.concatenate([m] * GROUP, axis=0), 0.0).astype(BF16)

    def prepare(r, kk, v, lw, kd, b, t_prev, masks, rev):
        same_head = masks[0]
        tri = jnp.where((si >= ti) if rev else (si <= ti), 1.0, 0.0).astype(BF16)
        lw_hi = lw.astype(BF16)
        lw_rest = lw - lw_hi.astype(F32)
        lw_mid = lw_rest.astype(BF16)
        lw_lo = (lw_rest - lw_mid.astype(F32)).astype(BF16)
        g_parts = jnp.dot(tri, jnp.concatenate([lw_hi, lw_mid, lw_lo], axis=1), preferred_element_type=F32)
        g_inc = g_parts[:, 0:n] + g_parts[:, n:2 * n] + g_parts[:, 2 * n:3 * n]
        g_exc = g_inc - lw
        g_end = g_inc[0:1, :] if rev else g_inc[CHUNK - 1:CHUNK, :]
        e_neg = jnp.exp(-g_inc)
        e_end = jnp.exp(g_end - g_inc)
        decay = jnp.exp(jnp.broadcast_to(g_end, (LANE, n)).T)
        return dict(kk=(kk * jnp.exp(g_exc)).astype(BF16), r=(r * jnp.exp(g_inc)).astype(BF16),
                    bk=jnp.concatenate([blk(b * e_neg, same_head), blk(kd * e_neg, same_head)], axis=0),
                    ends_t=jnp.concatenate([b * e_end, kd * e_end], axis=0).T.astype(BF16),
                    v=v.astype(BF16), v_blk=blk(v, same_head),
                    decay=jnp.concatenate([decay] * (n // LANE), axis=1), t=t_prev, t_b=t_prev.astype(BF16))

    ops = [prepare(*st) for st in streams]
    masks = [st[7] for st in streams]
    kr = [jnp.concatenate([o["kk"], o["r"]], axis=0) for o in ops]
    pairs = [_mm_nt(a, o["bk"]) for a, o in zip(kr, ops)]
    on_t = [_mm(a, o["t_b"]) for a, o in zip(kr, ops)]
    l_b = [jnp.where(m[1], p[0:CHUNK, 0:n], 0.0) for p, m in zip(pairs, masks)]
    a_rb = [jnp.where(m[2], p[CHUNK:2 * CHUNK, 0:n], 0.0).astype(BF16) for p, m in zip(pairs, masks)]
    on_v = [_mm(jnp.where(jnp.concatenate([m[1], m[2]], axis=0), p[:, n:2 * n], 0.0), o["v_blk"])
            for p, m, o in zip(pairs, masks, ops)]

    inv = [jnp.where(m[3], 1.0, 0.0) - jnp.where(m[4][0], lb, 0.0) for lb, m in zip(l_b, masks)]
    for lvl in range(1, len(masks[0][4])):
        half = [_mm(jnp.where(m[4][lvl], lb, 0.0), blk(a, m[0])) for lb, a, m in zip(l_b, inv, masks)]
        inv = [a - _mm(a, blk(h, m[0])) for a, h, m in zip(inv, half, masks)]

    x = [-(ot[0:CHUNK] + ov[0:CHUNK]) for ot, ov in zip(on_t, on_v)]
    u = [_mm(a, blk(xx, m[0])) for a, xx, m in zip(inv, x, masks)]
    ys = [ot[CHUNK:2 * CHUNK] + ov[CHUNK:2 * CHUNK] + _mm(ar, blk(uu, m[0]))
          for ot, ov, ar, uu, m in zip(on_t, on_v, a_rb, u, masks)]
    t_new = [o["decay"] * o["t"]
             + jnp.where(m[0], _mm(o["ends_t"], jnp.concatenate([uu.astype(BF16), o["v"]], axis=0)), 0.0)
             for o, uu, m in zip(ops, u, masks)]
    return ys, t_new


def _scan_kernel(shf_ref, shb_ref, df_ref, db_ref, yf_ref, yb_ref, s_ref, *, width):
    @pl.when(pl.program_id(1) == 0)
    def _():
        s_ref[...] = jnp.zeros_like(s_ref)

    gw = GROUP * HEAD
    n_groups = width // gw
    streams, sinks = [], []
    for d, (sh, dr, yo) in enumerate(((shf_ref, df_ref, yf_ref), (shb_ref, db_ref, yb_ref))):
        rev = d == 1
        masks = _scan_masks(rev)
        for j in range(sh.shape[0]):
            for g in range(n_groups):
                lo = g * gw
                idx = (d * sh.shape[0] + j) * n_groups + g
                streams.append((sh[j, :, lo:lo + gw], sh[j, :, width + lo:width + lo + gw],
                                sh[j, :, 2 * width + lo:2 * width + lo + gw],
                                dr[j, :, lo:lo + gw], dr[j, :, width + lo:width + lo + gw],
                                dr[j, :, 2 * width + lo:2 * width + lo + gw],
                                s_ref[idx], masks, rev))
                sinks.append((yo, j, lo, idx))
    ys, s_new = _scan_chunks(streams)
    for (yo, j, lo, idx), y, s in zip(sinks, ys, s_new):
        yo[j, :, lo:lo + gw] = y
        s_ref[idx] = s


def _wkv_scan(shared, per_dir, *, batch, ctx_chunks):
    ntok, w3 = shared.shape
    width = w3 // 3
    tok = ntok // batch
    nc = tok // CHUNK
    n_groups = width // (GROUP * HEAD)
    bb = SCAN_BATCH_BLOCK if batch % SCAN_BATCH_BLOCK == 0 else 1
    shared = shared.reshape(batch, tok, w3)
    per_dir = per_dir.reshape(2, batch, tok, w3)

    def bwd(i):
        return jnp.where(i < ctx_chunks, ctx_chunks - 1 - i, nc - 1 - (i - ctx_chunks))

    yf, yb = pl.pallas_call(
        functools.partial(_scan_kernel, width=width),
        grid=(batch // bb, nc),
        in_specs=[pl.BlockSpec((bb, CHUNK, w3), lambda b, i: (b, i, 0)),
                  pl.BlockSpec((bb, CHUNK, w3), lambda b, i: (b, bwd(i), 0)),
                  pl.BlockSpec((None, bb, CHUNK, w3), lambda b, i: (0, b, i, 0)),
                  pl.BlockSpec((None, bb, CHUNK, w3), lambda b, i: (1, b, bwd(i), 0))],
        out_specs=[pl.BlockSpec((bb, CHUNK, width), lambda b, i: (b, i, 0)),
                   pl.BlockSpec((bb, CHUNK, width), lambda b, i: (b, bwd(i), 0))],
        out_shape=[jax.ShapeDtypeStruct((batch, tok, width), F32)] * 2,
        scratch_shapes=[pltpu.VMEM((2 * bb * n_groups, GROUP * HEAD, GROUP * HEAD), F32)],
        compiler_params=_params(("parallel", "arbitrary")),
        name="wkv_scan",
    )(shared, shared, per_dir, per_dir)
    return yf.reshape(ntok, width), yb.reshape(ntok, width)


def _top2_route(logits):
    lane = lax.broadcasted_iota(jnp.int32, logits.shape, 1)
    n = logits.shape[-1]
    m1 = jnp.max(logits, axis=-1, keepdims=True)
    i1 = jnp.min(jnp.where(logits == m1, lane, n), axis=-1, keepdims=True)
    rest = jnp.where(lane == i1, PAD_LOGIT, logits)
    m2 = jnp.max(rest, axis=-1, keepdims=True)
    i2 = jnp.min(jnp.where(rest == m2, lane, n), axis=-1, keepdims=True)
    e = jnp.exp(m2 - m1)
    rec = jnp.where(lane == ROUTE_W2, e / (1.0 + e), 0.0)
    rec = jnp.where(lane == ROUTE_W1, 1.0 / (1.0 + e), rec)
    rec = jnp.where(lane == ROUTE_E2, i2.astype(F32), rec)
    return jnp.where(lane == ROUTE_E1, i1.astype(F32), rec)


def _outproj_kernel(*refs, width, alpha, moe):
    if moe:
        (yf_ref, yb_ref, ex_ref, x_ref, mod_ref, lg_ref, lb_ref, wo_ref, g1_ref, b1_ref, ones_ref,
         rw_ref, rb_ref, x1_ref, h2_ref, gate_ref) = refs
    else:
        (yf_ref, yb_ref, ex_ref, x_ref, mod_ref, lg_ref, lb_ref, wo_ref, g1_ref, b1_ref, ones_ref,
         x1_ref, h2_ref) = refs
    ones = ones_ref[...]
    y = yf_ref[...] + yb_ref[...]
    inv_n = 1.0 / HEAD
    mu = _mm(y, ones) * inv_n
    dlt = y - mu
    var = _mm(dlt * dlt, ones) * inv_n
    yn = dlt * lax.rsqrt(var + GN_EPS)
    bonus = ex_ref[:, 0:width]
    gate = ex_ref[:, width:2 * width]
    conv = ex_ref[:, 2 * width:3 * width]
    rwkv = (yn * lg_ref[...] + lb_ref[...] + bonus) * gate
    cat = jnp.concatenate([rwkv.astype(BF16), conv.astype(BF16)], axis=1)
    mix = jnp.dot(cat, wo_ref[...], preferred_element_type=F32)
    x1 = _layer_norm(alpha * x_ref[...] + mod_ref[2:3, :] * mix, g1_ref[...], b1_ref[...])
    h2 = x1 * (1.0 + mod_ref[4:5, :]) + mod_ref[3:4, :]
    x1_ref[...] = x1
    h2_ref[...] = h2.astype(h2_ref.dtype)
    if moe:
        h_hi = h2.astype(BF16)
        h_lo = (h2 - h_hi.astype(F32)).astype(BF16)
        rw = rw_ref[...]
        w_hi = rw.astype(BF16)
        w_lo = (rw - w_hi.astype(F32)).astype(BF16)
        logits = jnp.dot(jnp.concatenate([h_hi, h_hi, h_lo], axis=1), jnp.concatenate([w_hi, w_lo, w_hi], axis=0),
                         preferred_element_type=F32) + rb_ref[...]
        gate_ref[...] = _top2_route(logits)


def _outproj(yf, yb, extras, x, modtab, lnx_g, lnx_b, w_out, ln_g, ln_b, ones, router, *, alpha,
             blocks_per_batch):
    ntok, d = x.shape
    width = yf.shape[1]
    nb = ntok // TOK_BLOCK
    moe = router is not None
    const = lambda *shape: pl.BlockSpec(shape, lambda i: (0,) * len(shape))
    row = lambda cols: pl.BlockSpec((TOK_BLOCK, cols), lambda i: (i, 0))
    in_specs = [row(width), row(width), row(3 * width), row(d),
                pl.BlockSpec((None, None, 6, d),
                             lambda i: (i // blocks_per_batch, jnp.minimum(i % blocks_per_batch, 1), 0, 0)),
                const(1, width), const(1, width), const(2 * width, d), const(1, d), const(1, d),
                const(width, width)]
    out_specs = [row(d), row(d)]
    out_shape = [jax.ShapeDtypeStruct((ntok, d), F32), jax.ShapeDtypeStruct((ntok, d), F32 if moe else BF16)]
    args = [yf, yb, extras, x, modtab, lnx_g, lnx_b, w_out, ln_g, ln_b, ones]
    if moe:
        in_specs += [const(d, LANE), const(1, LANE)]
        out_specs.append(row(LANE))
        out_shape.append(jax.ShapeDtypeStruct((ntok, LANE), F32))
        args += list(router)
    return pl.pallas_call(
        functools.partial(_outproj_kernel, width=width, alpha=alpha, moe=moe),
        grid=(nb,), in_specs=in_specs, out_specs=out_specs, out_shape=out_shape,
        compiler_params=_params(("parallel",)),
        name="readout_outproj",
    )(*args)


def _swiglu_tile(h, w1_ref, w3_ref, w2_ref):
    a = jnp.dot(h, w1_ref[...], preferred_element_type=F32)
    b = jnp.dot(h, w3_ref[...], preferred_element_type=F32)
    s = a * _sigmoid(a) * b
    return jnp.dot(s.astype(BF16), w2_ref[...], preferred_element_type=F32)


def _ffn_kernel(h_ref, x1_ref, w1_ref, w3_ref, w2_ref, mod_ref, g_ref, b_ref, o_ref, acc_ref, *, alpha, ctx_len,
                m_blocks_per_batch):
    f = pl.program_id(1)

    @pl.when(f == 0)
    def _():
        acc_ref[...] = jnp.zeros_like(acc_ref)

    acc_ref[...] += _swiglu_tile(h_ref[...], w1_ref, w3_ref, w2_ref)

    @pl.when(f == pl.num_programs(1) - 1)
    def _():
        tm = acc_ref.shape[0]
        rows = lax.broadcasted_iota(jnp.int32, acc_ref.shape, 0)
        row_in_batch = rows + (pl.program_id(0) % m_blocks_per_batch) * tm
        gate2 = jnp.where(row_in_batch < ctx_len, mod_ref[0, 5:6, :], mod_ref[1, 5:6, :])
        o_ref[...] = _layer_norm(alpha * x1_ref[...] + gate2 * acc_ref[...], g_ref[...], b_ref[...])


def _ffn(h2, x1, w1, w3, w2, layer, modtab, ln_g, ln_b, *, alpha, ctx_len, batch):
    ntok, d = x1.shape
    dff = w2.shape[1]
    mpb = FFN_M_BLOCKS_PER_BATCH
    tm = ntok // batch // mpb
    tf = w1.shape[3]
    row = lambda cols: pl.BlockSpec((tm, cols), lambda m, f: (m, 0))
    const = lambda *shape: pl.BlockSpec(shape, lambda m, f: (0,) * len(shape))
    in_specs = [row(d), row(d),
                pl.BlockSpec((None, None, d, tf), lambda m, f: (layer, f, 0, 0)),
                pl.BlockSpec((None, None, d, tf), lambda m, f: (layer, f, 0, 0)),
                pl.BlockSpec((None, tf, d), lambda m, f: (layer, f, 0)),
                pl.BlockSpec((None, 2, 6, d), lambda m, f: (m // mpb, 0, 0, 0)),
                const(1, d), const(1, d)]
    return pl.pallas_call(
        functools.partial(_ffn_kernel, alpha=alpha, ctx_len=ctx_len, m_blocks_per_batch=mpb),
        grid=(ntok // tm, dff // tf),
        in_specs=in_specs, out_specs=row(d),
        out_shape=jax.ShapeDtypeStruct((ntok, d), F32),
        scratch_shapes=[pltpu.VMEM((tm, d), F32)],
        compiler_params=_params(("parallel", "arbitrary")),
        name="dense_swiglu",
    )(h2, x1, w1, w3, w2, modtab, ln_g, ln_b)


def _route_plan(route, n_exp):
    ntok = route.shape[0]
    tm = MOE_ROW_TILE
    n_tiles_max = -(-(TOP_K * ntok + n_exp * (tm - 1)) // tm)
    expert = jnp.concatenate([route[:, ROUTE_E1], route[:, ROUTE_E2]]).astype(jnp.int32)
    onehot = (expert[:, None] == jnp.arange(n_exp, dtype=jnp.int32)[None, :]).astype(jnp.int32)
    running = jnp.cumsum(onehot, axis=0)
    rank = jnp.sum(running * onehot, axis=1) - 1
    tiles = (running[-1] + tm - 1) // tm
    tile_end = jnp.cumsum(tiles)
    slot = (tile_end - tiles)[expert] * tm + rank
    n_tiles = tile_end[-1]
    tile_ids = jnp.minimum(jnp.arange(n_tiles_max, dtype=jnp.int32), n_tiles - 1)
    tile_expert = jnp.sum((tile_ids[:, None] >= tile_end[None, :]).astype(jnp.int32), axis=1)
    token = jnp.arange(TOP_K * ntok, dtype=jnp.int32) % ntok
    slot_token = jnp.zeros((n_tiles_max * tm,), jnp.int32).at[slot].set(token)
    return tile_expert, n_tiles.reshape(1), slot_token, slot


def _moe_experts_kernel(te_ref, nt_ref, tok_ref, h_hbm, w1_ref, w3_ref, w2_ref, o_ref, rows_ref, hb_ref, acc_ref,
                        sems, *, nf):
    t = pl.program_id(0)
    f = pl.program_id(1)
    tm = hb_ref.shape[0]
    per_step = tm // nf
    n_used = nt_ref[0]
    used = t < n_used
    cur = t % 2

    def row_copy(buf, r, src_row):
        return pltpu.make_async_copy(h_hbm.at[pl.ds(src_row, 1)], rows_ref.at[buf, pl.ds(r, 1)], sems.at[buf])

    @pl.when((t == 0) & (f == 0))
    def _():
        def start(r, carry):
            row_copy(0, r, tok_ref[r]).start()
            return carry

        lax.fori_loop(0, tm, start, 0, unroll=DMA_LOOP_UNROLL)

    @pl.when(used & (f == 0))
    def _():
        def wait(r, carry):
            row_copy(cur, r, 0).wait()
            return carry

        lax.fori_loop(0, tm, wait, 0, unroll=DMA_LOOP_UNROLL)
        hb_ref[...] = rows_ref[cur].astype(BF16)
        acc_ref[...] = jnp.zeros_like(acc_ref)

    has_next = t + 1 < n_used

    @pl.when(used & has_next)
    def _():
        base = f * per_step
        for j in range(per_step):
            row_copy(1 - cur, base + j, tok_ref[(t + 1) * tm + base + j]).start()
        acc_ref[...] += _swiglu_tile(hb_ref[...], w1_ref, w3_ref, w2_ref)

    @pl.when(used & jnp.logical_not(has_next))
    def _():
        acc_ref[...] += _swiglu_tile(hb_ref[...], w1_ref, w3_ref, w2_ref)

    @pl.when(f == pl.num_programs(1) - 1)
    def _():
        @pl.when(used)
        def _():
            o_ref[...] = acc_ref[...]

        @pl.when(jnp.logical_not(used))
        def _():
            o_ref[...] = jnp.zeros_like(o_ref)


def _moe_experts(h2, plan, w1, w3, w2, layer):
    tile_expert, n_tiles, slot_token, _ = plan
    ntok, d = h2.shape
    nf, tf = w1.shape[2], w1.shape[4]
    tm = MOE_ROW_TILE
    assert tm % nf == 0
    f_of = lambda t, f, nt: jnp.where(t < nt[0], f, nf - 1)
    grid_spec = pltpu.PrefetchScalarGridSpec(
        num_scalar_prefetch=3,
        grid=(tile_expert.shape[0], nf),
        in_specs=[pl.BlockSpec(memory_space=pl.ANY),
                  pl.BlockSpec((None, None, None, d, tf),
                               lambda t, f, te, nt, tk: (layer, te[t], f_of(t, f, nt), 0, 0)),
                  pl.BlockSpec((None, None, None, d, tf),
                               lambda t, f, te, nt, tk: (layer, te[t], f_of(t, f, nt), 0, 0)),
                  pl.BlockSpec((None, None, tf, d), lambda t, f, te, nt, tk: (layer, te[t], f_of(t, f, nt), 0))],
        out_specs=pl.BlockSpec((tm, d), lambda t, f, te, nt, tk: (t, 0)),
        scratch_shapes=[pltpu.VMEM((2, tm, d), F32), pltpu.VMEM((tm, d), BF16), pltpu.VMEM((tm, d), F32),
                        pltpu.SemaphoreType.DMA((2,))])
    return pl.pallas_call(
        functools.partial(_moe_experts_kernel, nf=nf), grid_spec=grid_spec,
        out_shape=jax.ShapeDtypeStruct((slot_token.shape[0], d), F32),
        compiler_params=_params(("arbitrary", "arbitrary")),
        name="moe_experts",
    )(tile_expert, n_tiles, slot_token, h2, w1, w3, w2)


def _moe_combine_kernel(slot_ref, y_hbm, x1_ref, route_ref, mod_ref, g_ref, b_ref, o_ref, rows_ref, sems, *, alpha,
                        ntok):
    i = pl.program_id(0)
    tb = x1_ref.shape[0]
    cur = i % 2

    def row_copy(buf, k, r, src_row):
        return pltpu.make_async_copy(y_hbm.at[pl.ds(src_row, 1)], rows_ref.at[buf, k, pl.ds(r, 1)], sems.at[buf])

    def start_block(block, buf):
        def start(r, carry):
            for k in range(TOP_K):
                row_copy(buf, k, r, slot_ref[k * ntok + block * tb + r]).start()
            return carry

        lax.fori_loop(0, tb, start, 0, unroll=DMA_LOOP_UNROLL)

    @pl.when(i == 0)
    def _():
        start_block(0, 0)

    @pl.when(i + 1 < pl.num_programs(0))
    def _():
        start_block(i + 1, 1 - cur)

    def wait(r, carry):
        for k in range(TOP_K):
            row_copy(cur, k, r, 0).wait()
        return carry

    lax.fori_loop(0, tb, wait, 0, unroll=DMA_LOOP_UNROLL)
    mix = (route_ref[:, ROUTE_W1:ROUTE_W1 + 1] * rows_ref[cur, 0]
           + route_ref[:, ROUTE_W2:ROUTE_W2 + 1] * rows_ref[cur, 1])
    o_ref[...] = _layer_norm(alpha * x1_ref[...] + mod_ref[5:6, :] * mix, g_ref[...], b_ref[...])


def _moe_combine(y_slots, plan, x1, route, modtab, ln_g, ln_b, *, alpha, blocks_per_batch):
    ntok, d = x1.shape
    slot = plan[3]
    row = lambda cols: pl.BlockSpec((TOK_BLOCK, cols), lambda i, s: (i, 0))
    const = lambda *shape: pl.BlockSpec(shape, lambda i, s: (0,) * len(shape))
    grid_spec = pltpu.PrefetchScalarGridSpec(
        num_scalar_prefetch=1,
        grid=(ntok // TOK_BLOCK,),
        in_specs=[pl.BlockSpec(memory_space=pl.ANY), row(d), row(LANE),
                  pl.BlockSpec((None, None, 6, d),
                               lambda i, s: (i // blocks_per_batch, jnp.minimum(i % blocks_per_batch, 1), 0, 0)),
                  const(1, d), const(1, d)],
        out_specs=row(d),
        scratch_shapes=[pltpu.VMEM((2, TOP_K, TOK_BLOCK, d), F32), pltpu.SemaphoreType.DMA((2,))])
    return pl.pallas_call(
        functools.partial(_moe_combine_kernel, alpha=alpha, ntok=ntok), grid_spec=grid_spec,
        out_shape=jax.ShapeDtypeStruct((ntok, d), F32),
        compiler_params=_params(("arbitrary",)),
        name="moe_combine",
    )(slot, y_slots, x1, route, modtab, ln_g, ln_b)


def _head_ones(width):
    hid = jnp.arange(width) // HEAD
    return (hid[:, None] == hid[None, :]).astype(BF16)


def _cast_kernel(w_ref, o_ref):
    o_ref[...] = w_ref[...].astype(o_ref.dtype)


def _column_tiles(w):
    *lead, d, dff = w.shape
    tf = min(FFN_F_BLOCK, dff)
    nf = dff // tf
    n = 1
    for s in lead:
        n *= s
    out = pl.pallas_call(
        _cast_kernel,
        grid=(n, nf),
        in_specs=[pl.BlockSpec((None, d, tf), lambda i, f: (i, 0, f))],
        out_specs=pl.BlockSpec((None, None, d, tf), lambda i, f: (i, f, 0, 0)),
        out_shape=jax.ShapeDtypeStruct((n, nf, d, tf), BF16),
        compiler_params=_params(("parallel", "parallel")),
        name="cast_column_tiles",
    )(w.reshape(n, d, dff))
    return out.reshape(*lead, nf, d, tf)


def _stack_lora(w):
    _, rank, width = w.shape
    z = jnp.zeros((rank, width), w.dtype)
    return jnp.concatenate([jnp.concatenate([w[0], z], axis=1), jnp.concatenate([z, w[1]], axis=1)], axis=0)


def kernel(x, c, ctx, c_ctx, w_mod, b_mod, w_in, rk_w0, rk_w2, rk_a0, rk_a2, rk_kk, rk_ka, rk_rk, rk_g2,
           rk_lnx_g, rk_lnx_b, conv_w, w_out, ln1_g, ln1_b, ln2_g, ln2_b, ffn_w1, ffn_w3, ffn_w2,
           router_w, router_b, moe_w1, moe_w3, moe_w2):
    batch, seq, d = x.shape
    ctx_len = ctx.shape[1]
    depth = w_in.shape[0]
    width = rk_kk.shape[-1]
    n_exp = router_w.shape[-1]
    tok = ctx_len + seq
    assert ctx_len == TOK_BLOCK and seq % TOK_BLOCK == 0 and seq % GRID_W == 0 and batch < 8
    assert 2 * rk_w2.shape[2] == LANE and 2 * rk_a2.shape[2] == LANE and rk_g2.shape[1] == LANE
    assert width % (GROUP * HEAD) == 0 and n_exp <= LANE and tok % (FFN_M_BLOCKS_PER_BATCH * 16) == 0
    blocks_per_batch = tok // TOK_BLOCK
    alpha = (2 * depth) ** 0.25

    cc = jnp.concatenate([c, c_ctx[None, :], jnp.zeros((8 - batch - 1, d), F32)], axis=0)
    mods = _modulation(cc, w_mod, b_mod).reshape(depth, 8, 6, d)
    modtab = jnp.stack([jnp.broadcast_to(mods[:, batch][:, None], (depth, batch, 6, d)), mods[:, :batch]],
                       axis=2)

    xs = jnp.concatenate([ctx, x], axis=1).reshape(batch * tok, d)
    ones = _head_ones(width)
    ffn_w = [_column_tiles(ffn_w1), _column_tiles(ffn_w3), ffn_w2.astype(BF16)]
    moe_w = [_column_tiles(moe_w1), _column_tiles(moe_w3), moe_w2.astype(BF16)]
    for l in range(depth):
        shared, per_dir, extras = _inproj(
            xs, modtab[l], w_in[l].astype(BF16), _stack_lora(rk_w2[l]).astype(BF16),
            _stack_lora(rk_a2[l]).astype(BF16), rk_w0[l].reshape(1, 2 * width), rk_a0[l].reshape(1, 2 * width),
            rk_kk[l][None], rk_ka[l][None], rk_rk[l].reshape(1, width), rk_g2[l].astype(BF16), conv_w[l], ones,
            blocks_per_batch=blocks_per_batch)
        yf, yb = _wkv_scan(shared, per_dir, batch=batch, ctx_chunks=ctx_len // CHUNK)
        i = l // 2
        router = None
        if l % 2 == 1:
            router = (jnp.pad(router_w[i], ((0, 0), (0, LANE - n_exp))),
                      jnp.pad(router_b[i], (0, LANE - n_exp), constant_values=PAD_LOGIT)[None])
        res = _outproj(yf, yb, extras, xs, modtab[l], rk_lnx_g[l][None], rk_lnx_b[l][None],
                       w_out[l].astype(BF16), ln1_g[l][None], ln1_b[l][None], ones, router,
                       alpha=alpha, blocks_per_batch=blocks_per_batch)
        if l % 2 == 0:
            x1, h2 = res
            xs = _ffn(h2, x1, *ffn_w, i, modtab[l], ln2_g[l][None], ln2_b[l][None], alpha=alpha, ctx_len=ctx_len,
                      batch=batch)
        else:
            x1, h2, route = res
            plan = _route_plan(route, n_exp)
            y_slots = _moe_experts(h2, plan, *moe_w, i)
            xs = _moe_combine(y_slots, plan, x1, route, modtab[l], ln2_g[l][None], ln2_b[l][None],
                              alpha=alpha, blocks_per_batch=blocks_per_batch)
    return xs.reshape(batch, tok, d)[:, ctx_len:]
```

```python
import functools

import jax
import jax.numpy as jnp
from jax import lax
from jax.experimental import pallas as pl
from jax.experimental.pallas import tpu as pltpu

F32 = jnp.float32
BF16 = jnp.bfloat16
HIGHEST = lax.Precision.HIGHEST

HEAD = 64
GRID_W = 64
CONV_K = 3
TOP_K = 2
LN_EPS = 1e-5
GN_EPS = 64e-5
NORM_EPS = 1e-12
PAD_LOGIT = -1e30
ROUTE_E1, ROUTE_E2, ROUTE_W1, ROUTE_W2 = 0, 1, 2, 3
MOE_ROW_TILE = 896
DMA_LOOP_UNROLL = 8
MXU_WIDTH_V7X = 256
GROUP = MXU_WIDTH_V7X // HEAD
CHUNK = MXU_WIDTH_V7X // GROUP
SCAN_BATCH_BLOCK = 4
TOK_BLOCK = 256
LANE = 128
VMEM_LIMIT_V7X = 56 * 1024 * 1024
FFN_M_BLOCKS_PER_BATCH = 4
FFN_F_BLOCK = 512
MOD_N_BLOCK = 1536


def _mm(a, b):
    return jnp.dot(a.astype(BF16), b.astype(BF16), preferred_element_type=F32)


def _mm_nt(a, b):
    return lax.dot_general(a.astype(BF16), b.astype(BF16), (((1,), (1,)), ((), ())),
                           preferred_element_type=F32)


def _sigmoid(x):
    return 1.0 / (1.0 + jnp.exp(-x))


def _layer_norm(z, g, b):
    mu = jnp.mean(z, axis=-1, keepdims=True)
    d = z - mu
    var = jnp.mean(d * d, axis=-1, keepdims=True)
    return d * lax.rsqrt(var + LN_EPS) * g + b


def _params(sem):
    return pltpu.CompilerParams(dimension_semantics=sem, vmem_limit_bytes=VMEM_LIMIT_V7X)


def _mod_kernel(c_ref, w_ref, b_ref, o_ref):
    c = c_ref[...]
    s = c * _sigmoid(c)
    o_ref[...] = jnp.dot(s, w_ref[...], precision=HIGHEST, preferred_element_type=F32) + b_ref[...]


def _modulation(cc, w_mod, b_mod):
    depth, d, n = w_mod.shape
    tn = MOD_N_BLOCK
    return pl.pallas_call(
        _mod_kernel,
        grid=(depth, n // tn),
        in_specs=[pl.BlockSpec((8, d), lambda l, j: (0, 0)),
                  pl.BlockSpec((None, d, tn), lambda l, j: (l, 0, j)),
                  pl.BlockSpec((None, 1, tn), lambda l, j: (l, 0, j))],
        out_specs=pl.BlockSpec((None, 8, tn), lambda l, j: (l, 0, j)),
        out_shape=jax.ShapeDtypeStruct((depth, 8, n), F32),
        compiler_params=_params(("parallel", "parallel")),
        name="adaln_modulation",
    )(cc, w_mod, b_mod.reshape(depth, 1, n))


def _inproj_kernel(x_ref, mod_ref, w_ref, w2s_ref, a2s_ref, w0_ref, a0_ref, kk_ref, ka_ref, rk_ref, g2_ref,
                   cw_ref, ones_ref, sh_ref, dir_ref, ex_ref, *, width, blocks_per_batch):
    wd_ = width
    x = x_ref[...]
    h = x * (1.0 + mod_ref[1:2, :]) + mod_ref[0:1, :]
    p = _mm(h, w_ref[...])
    r = p[:, 0:wd_]
    k = p[:, wd_:2 * wd_]
    v = p[:, 2 * wd_:3 * wd_]
    o = 3 * wd_
    wd = p[:, o:o + LANE]
    ad = p[:, o + LANE:o + 2 * LANE]
    gd = p[:, o + 2 * LANE:o + 3 * LANE]
    o = o + 3 * LANE
    cb = p[:, o:o + wd_]
    cc = p[:, o + wd_:o + 2 * wd_]
    ch = p[:, o + 2 * wd_:o + 3 * wd_]
    ones = ones_ref[...]

    z = w0_ref[...] + _mm(jnp.tanh(wd), w2s_ref[...])
    softplus_neg = jnp.maximum(-z, 0.0) + jnp.log(1.0 + jnp.exp(-jnp.abs(z)))
    w_log = -softplus_neg - 0.5
    log_decay = -jnp.exp(w_log)
    a_all = _sigmoid(a0_ref[...] + _mm(ad, a2s_ref[...]))

    kk_raw = k * kk_ref[...]
    ss = _mm(kk_raw * kk_raw, ones)
    kk = kk_raw / jnp.maximum(jnp.sqrt(ss), NORM_EPS)

    sh_ref[:, 0:wd_] = r
    sh_ref[:, wd_:2 * wd_] = kk
    sh_ref[:, 2 * wd_:3 * wd_] = v
    kd_sum = jnp.zeros_like(k)
    for d in range(2):
        a = a_all[:, d * wd_:(d + 1) * wd_]
        kd = k * (1.0 + (a - 1.0) * ka_ref[...])
        kd_sum = kd_sum + kd
        dir_ref[d, :, 0:wd_] = log_decay[:, d * wd_:(d + 1) * wd_]
        dir_ref[d, :, wd_:2 * wd_] = kd
        dir_ref[d, :, 2 * wd_:3 * wd_] = kk * a
    bonus = _mm(r * kd_sum * rk_ref[...], ones) * v
    gate = _mm(_sigmoid(gd), g2_ref[...])

    u = cc * ch
    rows = lax.broadcasted_iota(jnp.int32, u.shape, 0)
    is_ctx = (pl.program_id(0) % blocks_per_batch) == 0
    period_mask = jnp.where(is_ctx, TOK_BLOCK - 1, GRID_W - 1)
    pos = rows & period_mask
    u_prev = jnp.where(pos == 0, 0.0, pltpu.roll(u, 1, 0))
    u_next = jnp.where(pos == period_mask, 0.0, pltpu.roll(u, TOK_BLOCK - 1, 0))
    conv = cb * (u_prev * cw_ref[0:1, :] + u * cw_ref[1:2, :] + u_next * cw_ref[2:3, :])

    ex_ref[:, 0:wd_] = bonus
    ex_ref[:, wd_:2 * wd_] = gate
    ex_ref[:, 2 * wd_:3 * wd_] = conv


def _inproj(x, modtab, w_in, w2s, a2s, w0, a0, k_k, k_a, r_k, g2, conv_w, ones, *, blocks_per_batch):
    ntok, d = x.shape
    width = k_k.shape[-1]
    pcols = w_in.shape[1]
    nb = ntok // TOK_BLOCK
    const = lambda *shape: pl.BlockSpec(shape, lambda i: (0,) * len(shape))
    return pl.pallas_call(
        functools.partial(_inproj_kernel, width=width, blocks_per_batch=blocks_per_batch),
        grid=(nb,),
        in_specs=[pl.BlockSpec((TOK_BLOCK, d), lambda i: (i, 0)),
                  pl.BlockSpec((None, None, 6, d),
                               lambda i: (i // blocks_per_batch, jnp.minimum(i % blocks_per_batch, 1), 0, 0)),
                  const(d, pcols), const(LANE, 2 * width), const(LANE, 2 * width), const(1, 2 * width),
                  const(1, 2 * width), const(1, width), const(1, width), const(1, width), const(LANE, width),
                  const(CONV_K, width), const(width, width)],
        out_specs=[pl.BlockSpec((TOK_BLOCK, 3 * width), lambda i: (i, 0)),
                   pl.BlockSpec((2, TOK_BLOCK, 3 * width), lambda i: (0, i, 0)),
                   pl.BlockSpec((TOK_BLOCK, 3 * width), lambda i: (i, 0))],
        out_shape=[jax.ShapeDtypeStruct((ntok, 3 * width), F32),
                   jax.ShapeDtypeStruct((2, ntok, 3 * width), F32),
                   jax.ShapeDtypeStruct((ntok, 3 * width), F32)],
        compiler_params=_params(("parallel",)),
        name="inproj_prep",
    )(x, modtab, w_in, w2s, a2s, w0, a0, k_k, k_a, r_k, g2, conv_w, ones)


def _scan_masks(rev):
    n = GROUP * CHUNK
    ri = lax.broadcasted_iota(jnp.int32, (n, n), 0)
    ci = lax.broadcasted_iota(jnp.int32, (n, n), 1)
    same_head = (ri // HEAD) == (ci // HEAD)
    t = lax.broadcasted_iota(jnp.int32, (CHUNK, n), 0)
    s = lax.broadcasted_iota(jnp.int32, (CHUNK, n), 1) % CHUNK
    before = (s > t) if rev else (s < t)
    levels = []
    size = 1
    while size < CHUNK:
        levels.append(((t // (2 * size)) == (s // (2 * size))) & ((t // size) != (s // size)))
        size *= 2
    return same_head, before, before | (s == t), s == t, levels


def _scan_chunks(streams):
    n = GROUP * CHUNK
    ti = lax.broadcasted_iota(jnp.int32, (CHUNK, CHUNK), 0)
    si = lax.broadcasted_iota(jnp.int32, (CHUNK, CHUNK), 1)

    def blk(m, same_head):
        return jnp.where(same_head, jnp.concatenate([m] * GROUP, axis=0), 0.0).astype(BF16)

    def prepare(r, kk, v, lw, kd, b, t_prev, masks, rev):
        same_head = masks[0]
        tri = jnp.where((si >= ti) if rev else (si <= ti), 1.0, 0.0).astype(BF16)
        lw_hi = lw.astype(BF16)
        lw_rest = lw - lw_hi.astype(F32)
        lw_mid = lw_rest.astype(BF16)
        lw_lo = (lw_rest - lw_mid.astype(F32)).astype(BF16)
        g_parts = jnp.dot(tri, jnp.concatenate([lw_hi, lw_mid, lw_lo], axis=1), preferred_element_type=F32)
        g_inc = g_parts[:, 0:n] + g_parts[:, n:2 * n] + g_parts[:, 2 * n:3 * n]
        g_exc = g_inc - lw
        g_end = g_inc[0:1, :] if rev else g_inc[CHUNK - 1:CHUNK, :]
        e_neg = jnp.exp(-g_inc)
        e_end = jnp.exp(g_end - g_inc)
        decay = jnp.exp(jnp.broadcast_to(g_end, (LANE, n)).T)
        return dict(kk=(kk * jnp.exp(g_exc)).astype(BF16), r=(r * jnp.exp(g_inc)).astype(BF16),
                    bk=jnp.concatenate([blk(b * e_neg, same_head), blk(kd * e_neg, same_head)], axis=0),
                    ends_t=jnp.concatenate([b * e_end, kd * e_end], axis=0).T.astype(BF16),
                    v=v.astype(BF16), v_blk=blk(v, same_head),
                    decay=jnp.concatenate([decay] * (n // LANE), axis=1), t=t_prev, t_b=t_prev.astype(BF16))

    ops = [prepare(*st) for st in streams]
    masks = [st[7] for st in streams]
    kr = [jnp.concatenate([o["kk"], o["r"]], axis=0) for o in ops]
    pairs = [_mm_nt(a, o["bk"]) for a, o in zip(kr, ops)]
    on_t = [_mm(a, o["t_b"]) for a, o in zip(kr, ops)]
    l_b = [jnp.where(m[1], p[0:CHUNK, 0:n], 0.0) for p, m in zip(pairs, masks)]
    a_rb = [jnp.where(m[2], p[CHUNK:2 * CHUNK, 0:n], 0.0).astype(BF16) for p, m in zip(pairs, masks)]
    on_v = [_mm(jnp.where(jnp.concatenate([m[1], m[2]], axis=0), p[:, n:2 * n], 0.0), o["v_blk"])
            for p, m, o in zip(pairs, masks, ops)]

    inv = [jnp.where(m[3], 1.0, 0.0) - jnp.where(m[4][0], lb, 0.0) for lb, m in zip(l_b, masks)]
    for lvl in range(1, len(masks[0][4])):
        half = [_mm(jnp.where(m[4][lvl], lb, 0.0), blk(a, m[0])) for lb, a, m in zip(l_b, inv, masks)]
        inv = [a - _mm(a, blk(h, m[0])) for a, h, m in zip(inv, half, masks)]

    x = [-(ot[0:CHUNK] + ov[0:CHUNK]) for ot, ov in zip(on_t, on_v)]
    u = [_mm(a, blk(xx, m[0])) for a, xx, m in zip(inv, x, masks)]
    ys = [ot[CHUNK:2 * CHUNK] + ov[CHUNK:2 * CHUNK] + _mm(ar, blk(uu, m[0]))
          for ot, ov, ar, uu, m in zip(on_t, on_v, a_rb, u, masks)]
    t_new = [o["decay"] * o["t"]
             + jnp.where(m[0], _mm(o["ends_t"], jnp.concatenate([uu.astype(BF16), o["v"]], axis=0)), 0.0)
             for o, uu, m in zip(ops, u, masks)]
    return ys, t_new


def _scan_kernel(shf_ref, shb_ref, df_ref, db_ref, yf_ref, yb_ref, s_ref, *, width):
    @pl.when(pl.program_id(1) == 0)
    def _():
        s_ref[...] = jnp.zeros_like(s_ref)

    gw = GROUP * HEAD
    n_groups = width // gw
    streams, sinks = [], []
    for d, (sh, dr, yo) in enumerate(((shf_ref, df_ref, yf_ref), (shb_ref, db_ref, yb_ref))):
        rev = d == 1
        masks = _scan_masks(rev)
        for j in range(sh.shape[0]):
            for g in range(n_groups):
                lo = g * gw
                idx = (d * sh.shape[0] + j) * n_groups + g
                streams.append((sh[j, :, lo:lo + gw], sh[j, :, width + lo:width + lo + gw],
                                sh[j, :, 2 * width + lo:2 * width + lo + gw],
                                dr[j, :, lo:lo + gw], dr[j, :, width + lo:width + lo + gw],
                                dr[j, :, 2 * width + lo:2 * width + lo + gw],
                                s_ref[idx], masks, rev))
                sinks.append((yo, j, lo, idx))
    ys, s_new = _scan_chunks(streams)
    for (yo, j, lo, idx), y, s in zip(sinks, ys, s_new):
        yo[j, :, lo:lo + gw] = y
        s_ref[idx] = s


def _wkv_scan(shared, per_dir, *, batch, ctx_chunks):
    ntok, w3 = shared.shape
    width = w3 // 3
    tok = ntok // batch
    nc = tok // CHUNK
    n_groups = width // (GROUP * HEAD)
    bb = SCAN_BATCH_BLOCK if batch % SCAN_BATCH_BLOCK == 0 else 1
    shared = shared.reshape(batch, tok, w3)
    per_dir = per_dir.reshape(2, batch, tok, w3)

    def bwd(i):
        return jnp.where(i < ctx_chunks, ctx_chunks - 1 - i, nc - 1 - (i - ctx_chunks))

    yf, yb = pl.pallas_call(
        functools.partial(_scan_kernel, width=width),
        grid=(batch // bb, nc),
        in_specs=[pl.BlockSpec((bb, CHUNK, w3), lambda b, i: (b, i, 0)),
                  pl.BlockSpec((bb, CHUNK, w3), lambda b, i: (b, bwd(i), 0)),
                  pl.BlockSpec((None, bb, CHUNK, w3), lambda b, i: (0, b, i, 0)),
                  pl.BlockSpec((None, bb, CHUNK, w3), lambda b, i: (1, b, bwd(i), 0))],
        out_specs=[pl.BlockSpec((bb, CHUNK, width), lambda b, i: (b, i, 0)),
                   pl.BlockSpec((bb, CHUNK, width), lambda b, i: (b, bwd(i), 0))],
        out_shape=[jax.ShapeDtypeStruct((batch, tok, width), F32)] * 2,
        scratch_shapes=[pltpu.VMEM((2 * bb * n_groups, GROUP * HEAD, GROUP * HEAD), F32)],
        compiler_params=_params(("parallel", "arbitrary")),
        name="wkv_scan",
    )(shared, shared, per_dir, per_dir)
    return yf.reshape(ntok, width), yb.reshape(ntok, width)


def _top2_route(logits):
    lane = lax.broadcasted_iota(jnp.int32, logits.shape, 1)
    n = logits.shape[-1]
    m1 = jnp.max(logits, axis=-1, keepdims=True)
    i1 = jnp.min(jnp.where(logits == m1, lane, n), axis=-1, keepdims=True)
    rest = jnp.where(lane == i1, PAD_LOGIT, logits)
    m2 = jnp.max(rest, axis=-1, keepdims=True)
    i2 = jnp.min(jnp.where(rest == m2, lane, n), axis=-1, keepdims=True)
    e = jnp.exp(m2 - m1)
    rec = jnp.where(lane == ROUTE_W2, e / (1.0 + e), 0.0)
    rec = jnp.where(lane == ROUTE_W1, 1.0 / (1.0 + e), rec)
    rec = jnp.where(lane == ROUTE_E2, i2.astype(F32), rec)
    return jnp.where(lane == ROUTE_E1, i1.astype(F32), rec)


def _outproj_kernel(*refs, width, alpha, moe):
    if moe:
        (yf_ref, yb_ref, ex_ref, x_ref, mod_ref, lg_ref, lb_ref, wo_ref, g1_ref, b1_ref, ones_ref,
         rw_ref, rb_ref, x1_ref, h2_ref, gate_ref) = refs
    else:
        (yf_ref, yb_ref, ex_ref, x_ref, mod_ref, lg_ref, lb_ref, wo_ref, g1_ref, b1_ref, ones_ref,
         x1_ref, h2_ref) = refs
    ones = ones_ref[...]
    y = yf_ref[...] + yb_ref[...]
    inv_n = 1.0 / HEAD
    mu = _mm(y, ones) * inv_n
    dlt = y - mu
    var = _mm(dlt * dlt, ones) * inv_n
    yn = dlt * lax.rsqrt(var + GN_EPS)
    bonus = ex_ref[:, 0:width]
    gate = ex_ref[:, width:2 * width]
    conv = ex_ref[:, 2 * width:3 * width]
    rwkv = (yn * lg_ref[...] + lb_ref[...] + bonus) * gate
    cat = jnp.concatenate([rwkv.astype(BF16), conv.astype(BF16)], axis=1)
    mix = jnp.dot(cat, wo_ref[...], preferred_element_type=F32)
    x1 = _layer_norm(alpha * x_ref[...] + mod_ref[2:3, :] * mix, g1_ref[...], b1_ref[...])
    h2 = x1 * (1.0 + mod_ref[4:5, :]) + mod_ref[3:4, :]
    x1_ref[...] = x1
    h2_ref[...] = h2.astype(h2_ref.dtype)
    if moe:
        h_hi = h2.astype(BF16)
        h_lo = (h2 - h_hi.astype(F32)).astype(BF16)
        rw = rw_ref[...]
        w_hi = rw.astype(BF16)
        w_lo = (rw - w_hi.astype(F32)).astype(BF16)
        logits = jnp.dot(jnp.concatenate([h_hi, h_hi, h_lo], axis=1), jnp.concatenate([w_hi, w_lo, w_hi], axis=0),
                         preferred_element_type=F32) + rb_ref[...]
        gate_ref[...] = _top2_route(logits)


def _outproj(yf, yb, extras, x, modtab, lnx_g, lnx_b, w_out, ln_g, ln_b, ones, router, *, alpha,
             blocks_per_batch):
    ntok, d = x.shape
    width = yf.shape[1]
    nb = ntok // TOK_BLOCK
    moe = router is not None
    const = lambda *shape: pl.BlockSpec(shape, lambda i: (0,) * len(shape))
    row = lambda cols: pl.BlockSpec((TOK_BLOCK, cols), lambda i: (i, 0))
    in_specs = [row(width), row(width), row(3 * width), row(d),
                pl.BlockSpec((None, None, 6, d),
                             lambda i: (i // blocks_per_batch, jnp.minimum(i % blocks_per_batch, 1), 0, 0)),
                const(1, width), const(1, width), const(2 * width, d), const(1, d), const(1, d),
                const(width, width)]
    out_specs = [row(d), row(d)]
    out_shape = [jax.ShapeDtypeStruct((ntok, d), F32), jax.ShapeDtypeStruct((ntok, d), F32 if moe else BF16)]
    args = [yf, yb, extras, x, modtab, lnx_g, lnx_b, w_out, ln_g, ln_b, ones]
    if moe:
        in_specs += [const(d, LANE), const(1, LANE)]
        out_specs.append(row(LANE))
        out_shape.append(jax.ShapeDtypeStruct((ntok, LANE), F32))
        args += list(router)
    return pl.pallas_call(
        functools.partial(_outproj_kernel, width=width, alpha=alpha, moe=moe),
        grid=(nb,), in_specs=in_specs, out_specs=out_specs, out_shape=out_shape,
        compiler_params=_params(("parallel",)),
        name="readout_outproj",
    )(*args)


def _swiglu_tile(h, w1_ref, w3_ref, w2_ref):
    a = jnp.dot(h, w1_ref[...], preferred_element_type=F32)
    b = jnp.dot(h, w3_ref[...], preferred_element_type=F32)
    s = a * _sigmoid(a) * b
    return jnp.dot(s.astype(BF16), w2_ref[...], preferred_element_type=F32)


def _ffn_kernel(h_ref, x1_ref, w1_ref, w3_ref, w2_ref, mod_ref, g_ref, b_ref, o_ref, acc_ref, *, alpha, ctx_len,
                m_blocks_per_batch):
    f = pl.program_id(1)

    @pl.when(f == 0)
    def _():
        acc_ref[...] = jnp.zeros_like(acc_ref)

    acc_ref[...] += _swiglu_tile(h_ref[...], w1_ref, w3_ref, w2_ref)

    @pl.when(f == pl.num_programs(1) - 1)
    def _():
        tm = acc_ref.shape[0]
        rows = lax.broadcasted_iota(jnp.int32, acc_ref.shape, 0)
        row_in_batch = rows + (pl.program_id(0) % m_blocks_per_batch) * tm
        gate2 = jnp.where(row_in_batch < ctx_len, mod_ref[0, 5:6, :], mod_ref[1, 5:6, :])
        o_ref[...] = _layer_norm(alpha * x1_ref[...] + gate2 * acc_ref[...], g_ref[...], b_ref[...])


def _ffn(h2, x1, w1, w3, w2, layer, modtab, ln_g, ln_b, *, alpha, ctx_len, batch):
    ntok, d = x1.shape
    dff = w1.shape[2]
    mpb = FFN_M_BLOCKS_PER_BATCH
    tm = ntok // batch // mpb
    tf = min(FFN_F_BLOCK, dff)
    row = lambda cols: pl.BlockSpec((tm, cols), lambda m, f: (m, 0))
    const = lambda *shape: pl.BlockSpec(shape, lambda m, f: (0,) * len(shape))
    in_specs = [row(d), row(d),
                pl.BlockSpec((None, d, tf), lambda m, f: (layer, 0, f)),
                pl.BlockSpec((None, d, tf), lambda m, f: (layer, 0, f)),
                pl.BlockSpec((None, tf, d), lambda m, f: (layer, f, 0)),
                pl.BlockSpec((None, 2, 6, d), lambda m, f: (m // mpb, 0, 0, 0)),
                const(1, d), const(1, d)]
    return pl.pallas_call(
        functools.partial(_ffn_kernel, alpha=alpha, ctx_len=ctx_len, m_blocks_per_batch=mpb),
        grid=(ntok // tm, dff // tf),
        in_specs=in_specs, out_specs=row(d),
        out_shape=jax.ShapeDtypeStruct((ntok, d), F32),
        scratch_shapes=[pltpu.VMEM((tm, d), F32)],
        compiler_params=_params(("parallel", "arbitrary")),
        name="dense_swiglu",
    )(h2, x1, w1, w3, w2, modtab, ln_g, ln_b)


def _route_plan(route, n_exp):
    ntok = route.shape[0]
    tm = MOE_ROW_TILE
    n_tiles_max = -(-(TOP_K * ntok + n_exp * (tm - 1)) // tm)
    expert = jnp.concatenate([route[:, ROUTE_E1], route[:, ROUTE_E2]]).astype(jnp.int32)
    onehot = (expert[:, None] == jnp.arange(n_exp, dtype=jnp.int32)[None, :]).astype(jnp.int32)
    running = jnp.cumsum(onehot, axis=0)
    rank = jnp.sum(running * onehot, axis=1) - 1
    tiles = (running[-1] + tm - 1) // tm
    tile_end = jnp.cumsum(tiles)
    slot = (tile_end - tiles)[expert] * tm + rank
    n_tiles = tile_end[-1]
    tile_ids = jnp.minimum(jnp.arange(n_tiles_max, dtype=jnp.int32), n_tiles - 1)
    tile_expert = jnp.sum((tile_ids[:, None] >= tile_end[None, :]).astype(jnp.int32), axis=1)
    token = jnp.arange(TOP_K * ntok, dtype=jnp.int32) % ntok
    slot_token = jnp.zeros((n_tiles_max * tm,), jnp.int32).at[slot].set(token)
    return tile_expert, n_tiles.reshape(1), slot_token, slot


def _moe_experts_kernel(te_ref, nt_ref, tok_ref, h_hbm, w1_ref, w3_ref, w2_ref, o_ref, rows_ref, hb_ref, acc_ref,
                        sems, *, nf):
    t = pl.program_id(0)
    f = pl.program_id(1)
    tm = hb_ref.shape[0]
    per_step = tm // nf
    n_used = nt_ref[0]
    used = t < n_used
    cur = t % 2

    def row_copy(buf, r, src_row):
        return pltpu.make_async_copy(h_hbm.at[pl.ds(src_row, 1)], rows_ref.at[buf, pl.ds(r, 1)], sems.at[buf])

    @pl.when((t == 0) & (f == 0))
    def _():
        def start(r, carry):
            row_copy(0, r, tok_ref[r]).start()
            return carry

        lax.fori_loop(0, tm, start, 0, unroll=DMA_LOOP_UNROLL)

    @pl.when(used & (f == 0))
    def _():
        def wait(r, carry):
            row_copy(cur, r, 0).wait()
            return carry

        lax.fori_loop(0, tm, wait, 0, unroll=DMA_LOOP_UNROLL)
        hb_ref[...] = rows_ref[cur].astype(BF16)
        acc_ref[...] = jnp.zeros_like(acc_ref)

    has_next = t + 1 < n_used

    @pl.when(used & has_next)
    def _():
        base = f * per_step
        for j in range(per_step):
            row_copy(1 - cur, base + j, tok_ref[(t + 1) * tm + base + j]).start()
        acc_ref[...] += _swiglu_tile(hb_ref[...], w1_ref, w3_ref, w2_ref)

    @pl.when(used & jnp.logical_not(has_next))
    def _():
        acc_ref[...] += _swiglu_tile(hb_ref[...], w1_ref, w3_ref, w2_ref)

    @pl.when(f == pl.num_programs(1) - 1)
    def _():
        @pl.when(used)
        def _():
            o_ref[...] = acc_ref[...]

        @pl.when(jnp.logical_not(used))
        def _():
            o_ref[...] = jnp.zeros_like(o_ref)


def _moe_experts(h2, plan, w1, w3, w2, layer):
    tile_expert, n_tiles, slot_token, _ = plan
    ntok, d = h2.shape
    dff = w1.shape[3]
    tm = MOE_ROW_TILE
    tf = min(FFN_F_BLOCK, dff)
    nf = dff // tf
    assert tm % nf == 0
    f_of = lambda t, f, nt: jnp.where(t < nt[0], f, nf - 1)
    grid_spec = pltpu.PrefetchScalarGridSpec(
        num_scalar_prefetch=3,
        grid=(tile_expert.shape[0], nf),
        in_specs=[pl.BlockSpec(memory_space=pl.ANY),
                  pl.BlockSpec((None, None, d, tf), lambda t, f, te, nt, tk: (layer, te[t], 0, f_of(t, f, nt))),
                  pl.BlockSpec((None, None, d, tf), lambda t, f, te, nt, tk: (layer, te[t], 0, f_of(t, f, nt))),
                  pl.BlockSpec((None, None, tf, d), lambda t, f, te, nt, tk: (layer, te[t], f_of(t, f, nt), 0))],
        out_specs=pl.BlockSpec((tm, d), lambda t, f, te, nt, tk: (t, 0)),
        scratch_shapes=[pltpu.VMEM((2, tm, d), F32), pltpu.VMEM((tm, d), BF16), pltpu.VMEM((tm, d), F32),
                        pltpu.SemaphoreType.DMA((2,))])
    return pl.pallas_call(
        functools.partial(_moe_experts_kernel, nf=nf), grid_spec=grid_spec,
        out_shape=jax.ShapeDtypeStruct((slot_token.shape[0], d), F32),
        compiler_params=_params(("arbitrary", "arbitrary")),
        name="moe_experts",
    )(tile_expert, n_tiles, slot_token, h2, w1, w3, w2)


def _moe_combine_kernel(slot_ref, y_hbm, x1_ref, route_ref, mod_ref, g_ref, b_ref, o_ref, rows_ref, sems, *, alpha,
                        ntok):
    i = pl.program_id(0)
    tb = x1_ref.shape[0]
    cur = i % 2

    def row_copy(buf, k, r, src_row):
        return pltpu.make_async_copy(y_hbm.at[pl.ds(src_row, 1)], rows_ref.at[buf, k, pl.ds(r, 1)], sems.at[buf])

    def start_block(block, buf):
        def start(r, carry):
            for k in range(TOP_K):
                row_copy(buf, k, r, slot_ref[k * ntok + block * tb + r]).start()
            return carry

        lax.fori_loop(0, tb, start, 0, unroll=DMA_LOOP_UNROLL)

    @pl.when(i == 0)
    def _():
        start_block(0, 0)

    @pl.when(i + 1 < pl.num_programs(0))
    def _():
        start_block(i + 1, 1 - cur)

    def wait(r, carry):
        for k in range(TOP_K):
            row_copy(cur, k, r, 0).wait()
        return carry

    lax.fori_loop(0, tb, wait, 0, unroll=DMA_LOOP_UNROLL)
    mix = (route_ref[:, ROUTE_W1:ROUTE_W1 + 1] * rows_ref[cur, 0]
           + route_ref[:, ROUTE_W2:ROUTE_W2 + 1] * rows_ref[cur, 1])
    o_ref[...] = _layer_norm(alpha * x1_ref[...] + mod_ref[5:6, :] * mix, g_ref[...], b_ref[...])


def _moe_combine(y_slots, plan, x1, route, modtab, ln_g, ln_b, *, alpha, blocks_per_batch):
    ntok, d = x1.shape
    slot = plan[3]
    row = lambda cols: pl.BlockSpec((TOK_BLOCK, cols), lambda i, s: (i, 0))
    const = lambda *shape: pl.BlockSpec(shape, lambda i, s: (0,) * len(shape))
    grid_spec = pltpu.PrefetchScalarGridSpec(
        num_scalar_prefetch=1,
        grid=(ntok // TOK_BLOCK,),
        in_specs=[pl.BlockSpec(memory_space=pl.ANY), row(d), row(LANE),
                  pl.BlockSpec((None, None, 6, d),
                               lambda i, s: (i // blocks_per_batch, jnp.minimum(i % blocks_per_batch, 1), 0, 0)),
                  const(1, d), const(1, d)],
        out_specs=row(d),
        scratch_shapes=[pltpu.VMEM((2, TOP_K, TOK_BLOCK, d), F32), pltpu.SemaphoreType.DMA((2,))])
    return pl.pallas_call(
        functools.partial(_moe_combine_kernel, alpha=alpha, ntok=ntok), grid_spec=grid_spec,
        out_shape=jax.ShapeDtypeStruct((ntok, d), F32),
        compiler_params=_params(("arbitrary",)),
        name="moe_combine",
    )(slot, y_slots, x1, route, modtab, ln_g, ln_b)


def _head_ones(width):
    hid = jnp.arange(width) // HEAD
    return (hid[:, None] == hid[None, :]).astype(BF16)


def _stack_lora(w):
    _, rank, width = w.shape
    z = jnp.zeros((rank, width), w.dtype)
    return jnp.concatenate([jnp.concatenate([w[0], z], axis=1), jnp.concatenate([z, w[1]], axis=1)], axis=0)


def kernel(x, c, ctx, c_ctx, w_mod, b_mod, w_in, rk_w0, rk_w2, rk_a0, rk_a2, rk_kk, rk_ka, rk_rk, rk_g2,
           rk_lnx_g, rk_lnx_b, conv_w, w_out, ln1_g, ln1_b, ln2_g, ln2_b, ffn_w1, ffn_w3, ffn_w2,
           router_w, router_b, moe_w1, moe_w3, moe_w2):
    batch, seq, d = x.shape
    ctx_len = ctx.shape[1]
    depth = w_in.shape[0]
    width = rk_kk.shape[-1]
    n_exp = router_w.shape[-1]
    tok = ctx_len + seq
    assert ctx_len == TOK_BLOCK and seq % TOK_BLOCK == 0 and seq % GRID_W == 0 and batch < 8
    assert 2 * rk_w2.shape[2] == LANE and 2 * rk_a2.shape[2] == LANE and rk_g2.shape[1] == LANE
    assert width % (GROUP * HEAD) == 0 and n_exp <= LANE and tok % (FFN_M_BLOCKS_PER_BATCH * 16) == 0
    blocks_per_batch = tok // TOK_BLOCK
    alpha = (2 * depth) ** 0.25

    cc = jnp.concatenate([c, c_ctx[None, :], jnp.zeros((8 - batch - 1, d), F32)], axis=0)
    mods = _modulation(cc, w_mod, b_mod).reshape(depth, 8, 6, d)
    modtab = jnp.stack([jnp.broadcast_to(mods[:, batch][:, None], (depth, batch, 6, d)), mods[:, :batch]],
                       axis=2)

    xs = jnp.concatenate([ctx, x], axis=1).reshape(batch * tok, d)
    ones = _head_ones(width)
    ffn_w = [w.astype(BF16) for w in (ffn_w1, ffn_w3, ffn_w2)]
    moe_w = [w.astype(BF16) for w in (moe_w1, moe_w3, moe_w2)]
    for l in range(depth):
        shared, per_dir, extras = _inproj(
            xs, modtab[l], w_in[l].astype(BF16), _stack_lora(rk_w2[l]).astype(BF16),
            _stack_lora(rk_a2[l]).astype(BF16), rk_w0[l].reshape(1, 2 * width), rk_a0[l].reshape(1, 2 * width),
            rk_kk[l][None], rk_ka[l][None], rk_rk[l].reshape(1, width), rk_g2[l].astype(BF16), conv_w[l], ones,
            blocks_per_batch=blocks_per_batch)
        yf, yb = _wkv_scan(shared, per_dir, batch=batch, ctx_chunks=ctx_len // CHUNK)
        i = l // 2
        router = None
        if l % 2 == 1:
            router = (jnp.pad(router_w[i], ((0, 0), (0, LANE - n_exp))),
                      jnp.pad(router_b[i], (0, LANE - n_exp), constant_values=PAD_LOGIT)[None])
        res = _outproj(yf, yb, extras, xs, modtab[l], rk_lnx_g[l][None], rk_lnx_b[l][None],
                       w_out[l].astype(BF16), ln1_g[l][None], ln1_b[l][None], ones, router,
                       alpha=alpha, blocks_per_batch=blocks_per_batch)
        if l % 2 == 0:
            x1, h2 = res
            xs = _ffn(h2, x1, *ffn_w, i, modtab[l], ln2_g[l][None], ln2_b[l][None], alpha=alpha, ctx_len=ctx_len,
                      batch=batch)
        else:
            x1, h2, route = res
            plan = _route_plan(route, n_exp)
            y_slots = _moe_experts(h2, plan, *moe_w, i)
            xs = _moe_combine(y_slots, plan, x1, route, modtab[l], ln2_g[l][None], ln2_b[l][None],
                              alpha=alpha, blocks_per_batch=blocks_per_batch)
    return xs.reshape(batch, tok, d)[:, ctx_len:]
```

```python
import functools

import jax
import jax.numpy as jnp
from jax import lax
from jax.experimental import pallas as pl
from jax.experimental.pallas import tpu as pltpu

F32 = jnp.float32
BF16 = jnp.bfloat16
HIGHEST = lax.Precision.HIGHEST

HEAD = 64
GRID_W = 64
CONV_K = 3
TOP_K = 2
LN_EPS = 1e-5
GN_EPS = 64e-5
NORM_EPS = 1e-12
PAD_LOGIT = -1e30
ROUTE_E1, ROUTE_E2, ROUTE_W1, ROUTE_W2 = 0, 1, 2, 3
MOE_ROW_TILE = 448
DMA_LOOP_UNROLL = 8
MXU_WIDTH_V7X = 256
GROUP = MXU_WIDTH_V7X // HEAD
CHUNK = MXU_WIDTH_V7X // GROUP
SCAN_BATCH_BLOCK = 4
TOK_BLOCK = 256
LANE = 128
VMEM_LIMIT_V7X = 56 * 1024 * 1024
FFN_M_BLOCKS_PER_BATCH = 4
FFN_F_BLOCK = 512
MOD_N_BLOCK = 1536


def _mm(a, b):
    return jnp.dot(a.astype(BF16), b.astype(BF16), preferred_element_type=F32)


def _mm_nt(a, b):
    return lax.dot_general(a.astype(BF16), b.astype(BF16), (((1,), (1,)), ((), ())),
                           preferred_element_type=F32)


def _sigmoid(x):
    return 1.0 / (1.0 + jnp.exp(-x))


def _layer_norm(z, g, b):
    mu = jnp.mean(z, axis=-1, keepdims=True)
    d = z - mu
    var = jnp.mean(d * d, axis=-1, keepdims=True)
    return d * lax.rsqrt(var + LN_EPS) * g + b


def _params(sem):
    return pltpu.CompilerParams(dimension_semantics=sem, vmem_limit_bytes=VMEM_LIMIT_V7X)


def _mod_kernel(c_ref, w_ref, b_ref, o_ref):
    c = c_ref[...]
    s = c * _sigmoid(c)
    o_ref[...] = jnp.dot(s, w_ref[...], precision=HIGHEST, preferred_element_type=F32) + b_ref[...]


def _modulation(cc, w_mod, b_mod):
    depth, d, n = w_mod.shape
    tn = MOD_N_BLOCK
    return pl.pallas_call(
        _mod_kernel,
        grid=(depth, n // tn),
        in_specs=[pl.BlockSpec((8, d), lambda l, j: (0, 0)),
                  pl.BlockSpec((None, d, tn), lambda l, j: (l, 0, j)),
                  pl.BlockSpec((None, 1, tn), lambda l, j: (l, 0, j))],
        out_specs=pl.BlockSpec((None, 8, tn), lambda l, j: (l, 0, j)),
        out_shape=jax.ShapeDtypeStruct((depth, 8, n), F32),
        compiler_params=_params(("parallel", "parallel")),
        name="adaln_modulation",
    )(cc, w_mod, b_mod.reshape(depth, 1, n))


def _inproj_kernel(x_ref, mod_ref, w_ref, w2s_ref, a2s_ref, w0_ref, a0_ref, kk_ref, ka_ref, rk_ref, g2_ref,
                   cw_ref, ones_ref, sh_ref, dir_ref, ex_ref, *, width, blocks_per_batch):
    wd_ = width
    x = x_ref[...]
    h = x * (1.0 + mod_ref[1:2, :]) + mod_ref[0:1, :]
    p = _mm(h, w_ref[...])
    r = p[:, 0:wd_]
    k = p[:, wd_:2 * wd_]
    v = p[:, 2 * wd_:3 * wd_]
    o = 3 * wd_
    wd = p[:, o:o + LANE]
    ad = p[:, o + LANE:o + 2 * LANE]
    gd = p[:, o + 2 * LANE:o + 3 * LANE]
    o = o + 3 * LANE
    cb = p[:, o:o + wd_]
    cc = p[:, o + wd_:o + 2 * wd_]
    ch = p[:, o + 2 * wd_:o + 3 * wd_]
    ones = ones_ref[...]

    z = w0_ref[...] + _mm(jnp.tanh(wd), w2s_ref[...])
    softplus_neg = jnp.maximum(-z, 0.0) + jnp.log(1.0 + jnp.exp(-jnp.abs(z)))
    w_log = -softplus_neg - 0.5
    log_decay = -jnp.exp(w_log)
    a_all = _sigmoid(a0_ref[...] + _mm(ad, a2s_ref[...]))

    kk_raw = k * kk_ref[...]
    ss = _mm(kk_raw * kk_raw, ones)
    kk = kk_raw / jnp.maximum(jnp.sqrt(ss), NORM_EPS)

    sh_ref[:, 0:wd_] = r
    sh_ref[:, wd_:2 * wd_] = kk
    sh_ref[:, 2 * wd_:3 * wd_] = v
    kd_sum = jnp.zeros_like(k)
    for d in range(2):
        a = a_all[:, d * wd_:(d + 1) * wd_]
        kd = k * (1.0 + (a - 1.0) * ka_ref[...])
        kd_sum = kd_sum + kd
        dir_ref[d, :, 0:wd_] = log_decay[:, d * wd_:(d + 1) * wd_]
        dir_ref[d, :, wd_:2 * wd_] = kd
        dir_ref[d, :, 2 * wd_:3 * wd_] = kk * a
    bonus = _mm(r * kd_sum * rk_ref[...], ones) * v
    gate = _mm(_sigmoid(gd), g2_ref[...])

    u = cc * ch
    rows = lax.broadcasted_iota(jnp.int32, u.shape, 0)
    is_ctx = (pl.program_id(0) % blocks_per_batch) == 0
    period_mask = jnp.where(is_ctx, TOK_BLOCK - 1, GRID_W - 1)
    pos = rows & period_mask
    u_prev = jnp.where(pos == 0, 0.0, pltpu.roll(u, 1, 0))
    u_next = jnp.where(pos == period_mask, 0.0, pltpu.roll(u, TOK_BLOCK - 1, 0))
    conv = cb * (u_prev * cw_ref[0:1, :] + u * cw_ref[1:2, :] + u_next * cw_ref[2:3, :])

    ex_ref[:, 0:wd_] = bonus
    ex_ref[:, wd_:2 * wd_] = gate
    ex_ref[:, 2 * wd_:3 * wd_] = conv


def _inproj(x, modtab, w_in, w2s, a2s, w0, a0, k_k, k_a, r_k, g2, conv_w, ones, *, blocks_per_batch):
    ntok, d = x.shape
    width = k_k.shape[-1]
    pcols = w_in.shape[1]
    nb = ntok // TOK_BLOCK
    const = lambda *shape: pl.BlockSpec(shape, lambda i: (0,) * len(shape))
    return pl.pallas_call(
        functools.partial(_inproj_kernel, width=width, blocks_per_batch=blocks_per_batch),
        grid=(nb,),
        in_specs=[pl.BlockSpec((TOK_BLOCK, d), lambda i: (i, 0)),
                  pl.BlockSpec((None, None, 6, d),
                               lambda i: (i // blocks_per_batch, jnp.minimum(i % blocks_per_batch, 1), 0, 0)),
                  const(d, pcols), const(LANE, 2 * width), const(LANE, 2 * width), const(1, 2 * width),
                  const(1, 2 * width), const(1, width), const(1, width), const(1, width), const(LANE, width),
                  const(CONV_K, width), const(width, width)],
        out_specs=[pl.BlockSpec((TOK_BLOCK, 3 * width), lambda i: (i, 0)),
                   pl.BlockSpec((2, TOK_BLOCK, 3 * width), lambda i: (0, i, 0)),
                   pl.BlockSpec((TOK_BLOCK, 3 * width), lambda i: (i, 0))],
        out_shape=[jax.ShapeDtypeStruct((ntok, 3 * width), F32),
                   jax.ShapeDtypeStruct((2, ntok, 3 * width), F32),
                   jax.ShapeDtypeStruct((ntok, 3 * width), F32)],
        compiler_params=_params(("parallel",)),
        name="inproj_prep",
    )(x, modtab, w_in, w2s, a2s, w0, a0, k_k, k_a, r_k, g2, conv_w, ones)


def _scan_masks(rev):
    n = GROUP * CHUNK
    ri = lax.broadcasted_iota(jnp.int32, (n, n), 0)
    ci = lax.broadcasted_iota(jnp.int32, (n, n), 1)
    same_head = (ri // HEAD) == (ci // HEAD)
    t = lax.broadcasted_iota(jnp.int32, (CHUNK, n), 0)
    s = lax.broadcasted_iota(jnp.int32, (CHUNK, n), 1) % CHUNK
    before = (s > t) if rev else (s < t)
    levels = []
    size = 1
    while size < CHUNK:
        levels.append(((t // (2 * size)) == (s // (2 * size))) & ((t // size) != (s // size)))
        size *= 2
    return same_head, before, before | (s == t), s == t, levels


def _scan_chunks(streams):
    n = GROUP * CHUNK
    ti = lax.broadcasted_iota(jnp.int32, (CHUNK, CHUNK), 0)
    si = lax.broadcasted_iota(jnp.int32, (CHUNK, CHUNK), 1)

    def blk(m, same_head):
        return jnp.where(same_head, jnp.concatenate([m] * GROUP, axis=0), 0.0).astype(BF16)

    def prepare(r, kk, v, lw, kd, b, t_prev, masks, rev):
        same_head = masks[0]
        tri = jnp.where((si >= ti) if rev else (si <= ti), 1.0, 0.0).astype(BF16)
        lw_hi = lw.astype(BF16)
        lw_rest = lw - lw_hi.astype(F32)
        lw_mid = lw_rest.astype(BF16)
        lw_lo = (lw_rest - lw_mid.astype(F32)).astype(BF16)
        g_parts = jnp.dot(tri, jnp.concatenate([lw_hi, lw_mid, lw_lo], axis=1), preferred_element_type=F32)
        g_inc = g_parts[:, 0:n] + g_parts[:, n:2 * n] + g_parts[:, 2 * n:3 * n]
        g_exc = g_inc - lw
        g_end = g_inc[0:1, :] if rev else g_inc[CHUNK - 1:CHUNK, :]
        e_neg = jnp.exp(-g_inc)
        e_end = jnp.exp(g_end - g_inc)
        decay = jnp.exp(jnp.broadcast_to(g_end, (LANE, n)).T)
        return dict(kk=(kk * jnp.exp(g_exc)).astype(BF16), r=(r * jnp.exp(g_inc)).astype(BF16),
                    bk=jnp.concatenate([blk(b * e_neg, same_head), blk(kd * e_neg, same_head)], axis=0),
                    ends_t=jnp.concatenate([b * e_end, kd * e_end], axis=0).T.astype(BF16),
                    v=v.astype(BF16), v_blk=blk(v, same_head),
                    decay=jnp.concatenate([decay] * (n // LANE), axis=1), t=t_prev, t_b=t_prev.astype(BF16))

    ops = [prepare(*st) for st in streams]
    masks = [st[7] for st in streams]
    kr = [jnp.concatenate([o["kk"], o["r"]], axis=0) for o in ops]
    pairs = [_mm_nt(a, o["bk"]) for a, o in zip(kr, ops)]
    on_t = [_mm(a, o["t_b"]) for a, o in zip(kr, ops)]
    l_b = [jnp.where(m[1], p[0:CHUNK, 0:n], 0.0) for p, m in zip(pairs, masks)]
    a_rb = [jnp.where(m[2], p[CHUNK:2 * CHUNK, 0:n], 0.0).astype(BF16) for p, m in zip(pairs, masks)]
    on_v = [_mm(jnp.where(jnp.concatenate([m[1], m[2]], axis=0), p[:, n:2 * n], 0.0), o["v_blk"])
            for p, m, o in zip(pairs, masks, ops)]

    inv = [jnp.where(m[3], 1.0, 0.0) - jnp.where(m[4][0], lb, 0.0) for lb, m in zip(l_b, masks)]
    for lvl in range(1, len(masks[0][4])):
        half = [_mm(jnp.where(m[4][lvl], lb, 0.0), blk(a, m[0])) for lb, a, m in zip(l_b, inv, masks)]
        inv = [a - _mm(a, blk(h, m[0])) for a, h, m in zip(inv, half, masks)]

    x = [-(ot[0:CHUNK] + ov[0:CHUNK]) for ot, ov in zip(on_t, on_v)]
    u = [_mm(a, blk(xx, m[0])) for a, xx, m in zip(inv, x, masks)]
    ys = [ot[CHUNK:2 * CHUNK] + ov[CHUNK:2 * CHUNK] + _mm(ar, blk(uu, m[0]))
          for ot, ov, ar, uu, m in zip(on_t, on_v, a_rb, u, masks)]
    t_new = [o["decay"] * o["t"]
             + jnp.where(m[0], _mm(o["ends_t"], jnp.concatenate([uu.astype(BF16), o["v"]], axis=0)), 0.0)
             for o, uu, m in zip(ops, u, masks)]
    return ys, t_new


def _scan_kernel(shf_ref, shb_ref, df_ref, db_ref, yf_ref, yb_ref, s_ref, *, width):
    @pl.when(pl.program_id(1) == 0)
    def _():
        s_ref[...] = jnp.zeros_like(s_ref)

    gw = GROUP * HEAD
    n_groups = width // gw
    streams, sinks = [], []
    for d, (sh, dr, yo) in enumerate(((shf_ref, df_ref, yf_ref), (shb_ref, db_ref, yb_ref))):
        rev = d == 1
        masks = _scan_masks(rev)
        for j in range(sh.shape[0]):
            for g in range(n_groups):
                lo = g * gw
                idx = (d * sh.shape[0] + j) * n_groups + g
                streams.append((sh[j, :, lo:lo + gw], sh[j, :, width + lo:width + lo + gw],
                                sh[j, :, 2 * width + lo:2 * width + lo + gw],
                                dr[j, :, lo:lo + gw], dr[j, :, width + lo:width + lo + gw],
                                dr[j, :, 2 * width + lo:2 * width + lo + gw],
                                s_ref[idx], masks, rev))
                sinks.append((yo, j, lo, idx))
    ys, s_new = _scan_chunks(streams)
    for (yo, j, lo, idx), y, s in zip(sinks, ys, s_new):
        yo[j, :, lo:lo + gw] = y
        s_ref[idx] = s


def _wkv_scan(shared, per_dir, *, batch, ctx_chunks):
    ntok, w3 = shared.shape
    width = w3 // 3
    tok = ntok // batch
    nc = tok // CHUNK
    n_groups = width // (GROUP * HEAD)
    bb = SCAN_BATCH_BLOCK if batch % SCAN_BATCH_BLOCK == 0 else 1
    shared = shared.reshape(batch, tok, w3)
    per_dir = per_dir.reshape(2, batch, tok, w3)

    def bwd(i):
        return jnp.where(i < ctx_chunks, ctx_chunks - 1 - i, nc - 1 - (i - ctx_chunks))

    yf, yb = pl.pallas_call(
        functools.partial(_scan_kernel, width=width),
        grid=(batch // bb, nc),
        in_specs=[pl.BlockSpec((bb, CHUNK, w3), lambda b, i: (b, i, 0)),
                  pl.BlockSpec((bb, CHUNK, w3), lambda b, i: (b, bwd(i), 0)),
                  pl.BlockSpec((None, bb, CHUNK, w3), lambda b, i: (0, b, i, 0)),
                  pl.BlockSpec((None, bb, CHUNK, w3), lambda b, i: (1, b, bwd(i), 0))],
        out_specs=[pl.BlockSpec((bb, CHUNK, width), lambda b, i: (b, i, 0)),
                   pl.BlockSpec((bb, CHUNK, width), lambda b, i: (b, bwd(i), 0))],
        out_shape=[jax.ShapeDtypeStruct((batch, tok, width), F32)] * 2,
        scratch_shapes=[pltpu.VMEM((2 * bb * n_groups, GROUP * HEAD, GROUP * HEAD), F32)],
        compiler_params=_params(("parallel", "arbitrary")),
        name="wkv_scan",
    )(shared, shared, per_dir, per_dir)
    return yf.reshape(ntok, width), yb.reshape(ntok, width)


def _top2_route(logits):
    lane = lax.broadcasted_iota(jnp.int32, logits.shape, 1)
    n = logits.shape[-1]
    m1 = jnp.max(logits, axis=-1, keepdims=True)
    i1 = jnp.min(jnp.where(logits == m1, lane, n), axis=-1, keepdims=True)
    rest = jnp.where(lane == i1, PAD_LOGIT, logits)
    m2 = jnp.max(rest, axis=-1, keepdims=True)
    i2 = jnp.min(jnp.where(rest == m2, lane, n), axis=-1, keepdims=True)
    e = jnp.exp(m2 - m1)
    rec = jnp.where(lane == ROUTE_W2, e / (1.0 + e), 0.0)
    rec = jnp.where(lane == ROUTE_W1, 1.0 / (1.0 + e), rec)
    rec = jnp.where(lane == ROUTE_E2, i2.astype(F32), rec)
    return jnp.where(lane == ROUTE_E1, i1.astype(F32), rec)


def _outproj_kernel(*refs, width, alpha, moe):
    if moe:
        (yf_ref, yb_ref, ex_ref, x_ref, mod_ref, lg_ref, lb_ref, wo_ref, g1_ref, b1_ref, ones_ref,
         rw_ref, rb_ref, x1_ref, h2_ref, gate_ref) = refs
    else:
        (yf_ref, yb_ref, ex_ref, x_ref, mod_ref, lg_ref, lb_ref, wo_ref, g1_ref, b1_ref, ones_ref,
         x1_ref, h2_ref) = refs
    ones = ones_ref[...]
    y = yf_ref[...] + yb_ref[...]
    inv_n = 1.0 / HEAD
    mu = _mm(y, ones) * inv_n
    dlt = y - mu
    var = _mm(dlt * dlt, ones) * inv_n
    yn = dlt * lax.rsqrt(var + GN_EPS)
    bonus = ex_ref[:, 0:width]
    gate = ex_ref[:, width:2 * width]
    conv = ex_ref[:, 2 * width:3 * width]
    rwkv = (yn * lg_ref[...] + lb_ref[...] + bonus) * gate
    cat = jnp.concatenate([rwkv.astype(BF16), conv.astype(BF16)], axis=1)
    mix = jnp.dot(cat, wo_ref[...], preferred_element_type=F32)
    x1 = _layer_norm(alpha * x_ref[...] + mod_ref[2:3, :] * mix, g1_ref[...], b1_ref[...])
    h2 = x1 * (1.0 + mod_ref[4:5, :]) + mod_ref[3:4, :]
    x1_ref[...] = x1
    h2_ref[...] = h2.astype(h2_ref.dtype)
    if moe:
        h_hi = h2.astype(BF16)
        h_lo = (h2 - h_hi.astype(F32)).astype(BF16)
        rw = rw_ref[...]
        w_hi = rw.astype(BF16)
        w_lo = (rw - w_hi.astype(F32)).astype(BF16)
        logits = jnp.dot(jnp.concatenate([h_hi, h_hi, h_lo], axis=1), jnp.concatenate([w_hi, w_lo, w_hi], axis=0),
                         preferred_element_type=F32) + rb_ref[...]
        gate_ref[...] = _top2_route(logits)


def _outproj(yf, yb, extras, x, modtab, lnx_g, lnx_b, w_out, ln_g, ln_b, ones, router, *, alpha,
             blocks_per_batch):
    ntok, d = x.shape
    width = yf.shape[1]
    nb = ntok // TOK_BLOCK
    moe = router is not None
    const = lambda *shape: pl.BlockSpec(shape, lambda i: (0,) * len(shape))
    row = lambda cols: pl.BlockSpec((TOK_BLOCK, cols), lambda i: (i, 0))
    in_specs = [row(width), row(width), row(3 * width), row(d),
                pl.BlockSpec((None, None, 6, d),
                             lambda i: (i // blocks_per_batch, jnp.minimum(i % blocks_per_batch, 1), 0, 0)),
                const(1, width), const(1, width), const(2 * width, d), const(1, d), const(1, d),
                const(width, width)]
    out_specs = [row(d), row(d)]
    out_shape = [jax.ShapeDtypeStruct((ntok, d), F32), jax.ShapeDtypeStruct((ntok, d), F32 if moe else BF16)]
    args = [yf, yb, extras, x, modtab, lnx_g, lnx_b, w_out, ln_g, ln_b, ones]
    if moe:
        in_specs += [const(d, LANE), const(1, LANE)]
        out_specs.append(row(LANE))
        out_shape.append(jax.ShapeDtypeStruct((ntok, LANE), F32))
        args += list(router)
    return pl.pallas_call(
        functools.partial(_outproj_kernel, width=width, alpha=alpha, moe=moe),
        grid=(nb,), in_specs=in_specs, out_specs=out_specs, out_shape=out_shape,
        compiler_params=_params(("parallel",)),
        name="readout_outproj",
    )(*args)


def _swiglu_tile(h, w1_ref, w3_ref, w2_ref):
    a = jnp.dot(h, w1_ref[...], preferred_element_type=F32)
    b = jnp.dot(h, w3_ref[...], preferred_element_type=F32)
    s = a * _sigmoid(a) * b
    return jnp.dot(s.astype(BF16), w2_ref[...], preferred_element_type=F32)


def _ffn_kernel(h_ref, x1_ref, w1_ref, w3_ref, w2_ref, mod_ref, g_ref, b_ref, o_ref, acc_ref, *, alpha, ctx_len,
                m_blocks_per_batch):
    f = pl.program_id(1)

    @pl.when(f == 0)
    def _():
        acc_ref[...] = jnp.zeros_like(acc_ref)

    acc_ref[...] += _swiglu_tile(h_ref[...], w1_ref, w3_ref, w2_ref)

    @pl.when(f == pl.num_programs(1) - 1)
    def _():
        tm = acc_ref.shape[0]
        rows = lax.broadcasted_iota(jnp.int32, acc_ref.shape, 0)
        row_in_batch = rows + (pl.program_id(0) % m_blocks_per_batch) * tm
        gate2 = jnp.where(row_in_batch < ctx_len, mod_ref[0, 5:6, :], mod_ref[1, 5:6, :])
        o_ref[...] = _layer_norm(alpha * x1_ref[...] + gate2 * acc_ref[...], g_ref[...], b_ref[...])


def _ffn(h2, x1, w1, w3, w2, layer, modtab, ln_g, ln_b, *, alpha, ctx_len, batch):
    ntok, d = x1.shape
    dff = w1.shape[2]
    mpb = FFN_M_BLOCKS_PER_BATCH
    tm = ntok // batch // mpb
    tf = min(FFN_F_BLOCK, dff)
    row = lambda cols: pl.BlockSpec((tm, cols), lambda m, f: (m, 0))
    const = lambda *shape: pl.BlockSpec(shape, lambda m, f: (0,) * len(shape))
    in_specs = [row(d), row(d),
                pl.BlockSpec((None, d, tf), lambda m, f: (layer, 0, f)),
                pl.BlockSpec((None, d, tf), lambda m, f: (layer, 0, f)),
                pl.BlockSpec((None, tf, d), lambda m, f: (layer, f, 0)),
                pl.BlockSpec((None, 2, 6, d), lambda m, f: (m // mpb, 0, 0, 0)),
                const(1, d), const(1, d)]
    return pl.pallas_call(
        functools.partial(_ffn_kernel, alpha=alpha, ctx_len=ctx_len, m_blocks_per_batch=mpb),
        grid=(ntok // tm, dff // tf),
        in_specs=in_specs, out_specs=row(d),
        out_shape=jax.ShapeDtypeStruct((ntok, d), F32),
        scratch_shapes=[pltpu.VMEM((tm, d), F32)],
        compiler_params=_params(("parallel", "arbitrary")),
        name="dense_swiglu",
    )(h2, x1, w1, w3, w2, modtab, ln_g, ln_b)


def _route_plan(route, n_exp):
    ntok = route.shape[0]
    tm = MOE_ROW_TILE
    n_tiles_max = -(-(TOP_K * ntok + n_exp * (tm - 1)) // tm)
    expert = jnp.concatenate([route[:, ROUTE_E1], route[:, ROUTE_E2]]).astype(jnp.int32)
    onehot = (expert[:, None] == jnp.arange(n_exp, dtype=jnp.int32)[None, :]).astype(jnp.int32)
    running = jnp.cumsum(onehot, axis=0)
    rank = jnp.sum(running * onehot, axis=1) - 1
    tiles = (running[-1] + tm - 1) // tm
    tile_end = jnp.cumsum(tiles)
    slot = (tile_end - tiles)[expert] * tm + rank
    n_tiles = tile_end[-1]
    tile_ids = jnp.minimum(jnp.arange(n_tiles_max, dtype=jnp.int32), n_tiles - 1)
    tile_expert = jnp.sum((tile_ids[:, None] >= tile_end[None, :]).astype(jnp.int32), axis=1)
    token = jnp.arange(TOP_K * ntok, dtype=jnp.int32) % ntok
    slot_token = jnp.zeros((n_tiles_max * tm,), jnp.int32).at[slot].set(token)
    return tile_expert, n_tiles.reshape(1), slot_token, slot


def _moe_experts_kernel(te_ref, nt_ref, tok_ref, h_hbm, w1_ref, w3_ref, w2_ref, o_ref, rows_ref, hb_ref, acc_ref,
                        sems, *, nf):
    t = pl.program_id(0)
    f = pl.program_id(1)
    tm = hb_ref.shape[0]
    per_step = tm // nf
    n_used = nt_ref[0]
    used = t < n_used
    cur = t % 2

    def row_copy(buf, r, src_row):
        return pltpu.make_async_copy(h_hbm.at[pl.ds(src_row, 1)], rows_ref.at[buf, pl.ds(r, 1)], sems.at[buf])

    @pl.when((t == 0) & (f == 0))
    def _():
        def start(r, carry):
            row_copy(0, r, tok_ref[r]).start()
            return carry

        lax.fori_loop(0, tm, start, 0, unroll=DMA_LOOP_UNROLL)

    @pl.when(used & (f == 0))
    def _():
        def wait(r, carry):
            row_copy(cur, r, 0).wait()
            return carry

        lax.fori_loop(0, tm, wait, 0, unroll=DMA_LOOP_UNROLL)
        hb_ref[...] = rows_ref[cur].astype(BF16)
        acc_ref[...] = jnp.zeros_like(acc_ref)

    has_next = t + 1 < n_used

    @pl.when(used & has_next)
    def _():
        base = f * per_step
        for j in range(per_step):
            row_copy(1 - cur, base + j, tok_ref[(t + 1) * tm + base + j]).start()
        acc_ref[...] += _swiglu_tile(hb_ref[...], w1_ref, w3_ref, w2_ref)

    @pl.when(used & jnp.logical_not(has_next))
    def _():
        acc_ref[...] += _swiglu_tile(hb_ref[...], w1_ref, w3_ref, w2_ref)

    @pl.when(f == pl.num_programs(1) - 1)
    def _():
        @pl.when(used)
        def _():
            o_ref[...] = acc_ref[...]

        @pl.when(jnp.logical_not(used))
        def _():
            o_ref[...] = jnp.zeros_like(o_ref)


def _moe_experts(h2, plan, w1, w3, w2, layer):
    tile_expert, n_tiles, slot_token, _ = plan
    ntok, d = h2.shape
    dff = w1.shape[3]
    tm = MOE_ROW_TILE
    tf = min(FFN_F_BLOCK, dff)
    nf = dff // tf
    assert tm % nf == 0
    f_of = lambda t, f, nt: jnp.where(t < nt[0], f, nf - 1)
    grid_spec = pltpu.PrefetchScalarGridSpec(
        num_scalar_prefetch=3,
        grid=(tile_expert.shape[0], nf),
        in_specs=[pl.BlockSpec(memory_space=pl.ANY),
                  pl.BlockSpec((None, None, d, tf), lambda t, f, te, nt, tk: (layer, te[t], 0, f_of(t, f, nt))),
                  pl.BlockSpec((None, None, d, tf), lambda t, f, te, nt, tk: (layer, te[t], 0, f_of(t, f, nt))),
                  pl.BlockSpec((None, None, tf, d), lambda t, f, te, nt, tk: (layer, te[t], f_of(t, f, nt), 0))],
        out_specs=pl.BlockSpec((tm, d), lambda t, f, te, nt, tk: (t, 0)),
        scratch_shapes=[pltpu.VMEM((2, tm, d), F32), pltpu.VMEM((tm, d), BF16), pltpu.VMEM((tm, d), F32),
                        pltpu.SemaphoreType.DMA((2,))])
    return pl.pallas_call(
        functools.partial(_moe_experts_kernel, nf=nf), grid_spec=grid_spec,
        out_shape=jax.ShapeDtypeStruct((slot_token.shape[0], d), F32),
        compiler_params=_params(("arbitrary", "arbitrary")),
        name="moe_experts",
    )(tile_expert, n_tiles, slot_token, h2, w1, w3, w2)


def _moe_combine_kernel(slot_ref, y_hbm, x1_ref, route_ref, mod_ref, g_ref, b_ref, o_ref, rows_ref, sems, *, alpha,
                        ntok):
    i = pl.program_id(0)
    tb = x1_ref.shape[0]
    cur = i % 2

    def row_copy(buf, k, r, src_row):
        return pltpu.make_async_copy(y_hbm.at[pl.ds(src_row, 1)], rows_ref.at[buf, k, pl.ds(r, 1)], sems.at[buf])

    def start_block(block, buf):
        def start(r, carry):
            for k in range(TOP_K):
                row_copy(buf, k, r, slot_ref[k * ntok + block * tb + r]).start(priority=k % 2)
            return carry

        lax.fori_loop(0, tb, start, 0, unroll=DMA_LOOP_UNROLL)

    @pl.when(i == 0)
    def _():
        start_block(0, 0)

    @pl.when(i + 1 < pl.num_programs(0))
    def _():
        start_block(i + 1, 1 - cur)

    def wait(r, carry):
        for k in range(TOP_K):
            row_copy(cur, k, r, 0).wait()
        return carry

    lax.fori_loop(0, tb, wait, 0, unroll=DMA_LOOP_UNROLL)
    mix = (route_ref[:, ROUTE_W1:ROUTE_W1 + 1] * rows_ref[cur, 0]
           + route_ref[:, ROUTE_W2:ROUTE_W2 + 1] * rows_ref[cur, 1])
    o_ref[...] = _layer_norm(alpha * x1_ref[...] + mod_ref[5:6, :] * mix, g_ref[...], b_ref[...])


def _moe_combine(y_slots, plan, x1, route, modtab, ln_g, ln_b, *, alpha, blocks_per_batch):
    ntok, d = x1.shape
    slot = plan[3]
    row = lambda cols: pl.BlockSpec((TOK_BLOCK, cols), lambda i, s: (i, 0))
    const = lambda *shape: pl.BlockSpec(shape, lambda i, s: (0,) * len(shape))
    grid_spec = pltpu.PrefetchScalarGridSpec(
        num_scalar_prefetch=1,
        grid=(ntok // TOK_BLOCK,),
        in_specs=[pl.BlockSpec(memory_space=pl.ANY), row(d), row(LANE),
                  pl.BlockSpec((None, None, 6, d),
                               lambda i, s: (i // blocks_per_batch, jnp.minimum(i % blocks_per_batch, 1), 0, 0)),
                  const(1, d), const(1, d)],
        out_specs=row(d),
        scratch_shapes=[pltpu.VMEM((2, TOP_K, TOK_BLOCK, d), F32), pltpu.SemaphoreType.DMA((2,))])
    return pl.pallas_call(
        functools.partial(_moe_combine_kernel, alpha=alpha, ntok=ntok), grid_spec=grid_spec,
        out_shape=jax.ShapeDtypeStruct((ntok, d), F32),
        compiler_params=_params(("arbitrary",)),
        name="moe_combine",
    )(slot, y_slots, x1, route, modtab, ln_g, ln_b)


def _head_ones(width):
    hid = jnp.arange(width) // HEAD
    return (hid[:, None] == hid[None, :]).astype(BF16)


def _stack_lora(w):
    _, rank, width = w.shape
    z = jnp.zeros((rank, width), w.dtype)
    return jnp.concatenate([jnp.concatenate([w[0], z], axis=1), jnp.concatenate([z, w[1]], axis=1)], axis=0)


def kernel(x, c, ctx, c_ctx, w_mod, b_mod, w_in, rk_w0, rk_w2, rk_a0, rk_a2, rk_kk, rk_ka, rk_rk, rk_g2,
           rk_lnx_g, rk_lnx_b, conv_w, w_out, ln1_g, ln1_b, ln2_g, ln2_b, ffn_w1, ffn_w3, ffn_w2,
           router_w, router_b, moe_w1, moe_w3, moe_w2):
    batch, seq, d = x.shape
    ctx_len = ctx.shape[1]
    depth = w_in.shape[0]
    width = rk_kk.shape[-1]
    n_exp = router_w.shape[-1]
    tok = ctx_len + seq
    assert ctx_len == TOK_BLOCK and seq % TOK_BLOCK == 0 and seq % GRID_W == 0 and batch < 8
    assert 2 * rk_w2.shape[2] == LANE and 2 * rk_a2.shape[2] == LANE and rk_g2.shape[1] == LANE
    assert width % (GROUP * HEAD) == 0 and n_exp <= LANE and tok % (FFN_M_BLOCKS_PER_BATCH * 16) == 0
    blocks_per_batch = tok // TOK_BLOCK
    alpha = (2 * depth) ** 0.25

    cc = jnp.concatenate([c, c_ctx[None, :], jnp.zeros((8 - batch - 1, d), F32)], axis=0)
    mods = _modulation(cc, w_mod, b_mod).reshape(depth, 8, 6, d)
    modtab = jnp.stack([jnp.broadcast_to(mods[:, batch][:, None], (depth, batch, 6, d)), mods[:, :batch]],
                       axis=2)

    xs = jnp.concatenate([ctx, x], axis=1).reshape(batch * tok, d)
    ones = _head_ones(width)
    ffn_w = [w.astype(BF16) for w in (ffn_w1, ffn_w3, ffn_w2)]
    moe_w = [w.astype(BF16) for w in (moe_w1, moe_w3, moe_w2)]
    for l in range(depth):
        shared, per_dir, extras = _inproj(
            xs, modtab[l], w_in[l].astype(BF16), _stack_lora(rk_w2[l]).astype(BF16),
            _stack_lora(rk_a2[l]).astype(BF16), rk_w0[l].reshape(1, 2 * width), rk_a0[l].reshape(1, 2 * width),
            rk_kk[l][None], rk_ka[l][None], rk_rk[l].reshape(1, width), rk_g2[l].astype(BF16), conv_w[l], ones,
            blocks_per_batch=blocks_per_batch)
        yf, yb = _wkv_scan(shared, per_dir, batch=batch, ctx_chunks=ctx_len // CHUNK)
        i = l // 2
        router = None
        if l % 2 == 1:
            router = (jnp.pad(router_w[i], ((0, 0), (0, LANE - n_exp))),
                      jnp.pad(router_b[i], (0, LANE - n_exp), constant_values=PAD_LOGIT)[None])
        res = _outproj(yf, yb, extras, xs, modtab[l], rk_lnx_g[l][None], rk_lnx_b[l][None],
                       w_out[l].astype(BF16), ln1_g[l][None], ln1_b[l][None], ones, router,
                       alpha=alpha, blocks_per_batch=blocks_per_batch)
        if l % 2 == 0:
            x1, h2 = res
            xs = _ffn(h2, x1, *ffn_w, i, modtab[l], ln2_g[l][None], ln2_b[l][None], alpha=alpha, ctx_len=ctx_len,
                      batch=batch)
        else:
            x1, h2, route = res
            plan = _route_plan(route, n_exp)
            y_slots = _moe_experts(h2, plan, *moe_w, i)
            xs = _moe_combine(y_slots, plan, x1, route, modtab[l], ln2_g[l][None], ln2_b[l][None],
                              alpha=alpha, blocks_per_batch=blocks_per_batch)
    return xs.reshape(batch, tok, d)[:, ctx_len:]
```
